```python
import functools
import jax, jax.numpy as jnp
from jax import lax
import numpy as np

D_MODEL = 4096
BATCH = 4
SEQ = 2048
DEPTH = 4
DEC_BATCH = 32
DEC_SEQ = 4
PAST_LEN = 8192
PAGE_SIZE = 128

N_MIXERS = 3
N_LAYERS_A = (DEPTH + 2) // 3
N_LAYERS_B = (DEPTH + 1) // 3
N_LAYERS_C = DEPTH // 3

GLA_HEADS = 8
GLA_DK = D_MODEL // (2 * GLA_HEADS)
GLA_DV = D_MODEL // GLA_HEADS
GLA_RANK = 16
GLA_TAU = 16.0
GLA_CHUNK = 64

RET_HEADS = 16
RET_DK = D_MODEL // RET_HEADS
RET_DV = 2 * RET_DK
RET_THETA = 10000.0
RET_CHUNK = 128

SWA_HEADS = 64
SWA_KV_HEADS = 8
SWA_HD = D_MODEL // SWA_HEADS
SWA_GROUP = SWA_HEADS // SWA_KV_HEADS
SWA_WINDOW = 128
ROPE_THETA = 500000.0
ROPE_DIMS = SWA_HD // 4

D_FF = 4 * D_MODEL
PLE_DIM = 256
NORM_EPS = 1e-6

kernel_name = 'hybrid_gla_retnet_swa_decoder_step'


def rmsnorm(x, g):
    xf = x.astype(jnp.float32)
    y = xf * lax.rsqrt(jnp.mean(xf * xf, axis=-1, keepdims=True) + NORM_EPS)
    return (y * g.astype(jnp.float32)).astype(x.dtype)


def rotary(x, pos, n_rot, theta):
    half = n_rot // 2
    inv_freq = theta ** (-jnp.arange(half, dtype=jnp.float32) / half)
    ang = pos.astype(jnp.float32)[:, None] * inv_freq[None, :]
    cos = jnp.cos(ang)[:, None, :]
    sin = jnp.sin(ang)[:, None, :]
    xf = x.astype(jnp.float32)
    x1 = xf[..., :half]
    x2 = xf[..., half:n_rot]
    out = jnp.concatenate([x1 * cos - x2 * sin, x2 * cos + x1 * sin, xf[..., n_rot:]], axis=-1)
    return out.astype(x.dtype)


def chunk_size(length, preferred):
    return preferred if length % preferred == 0 else length


def to_chunks(t, c):
    b, l, h, d = t.shape
    return t.reshape(b, l // c, c, h, d).transpose(1, 0, 3, 2, 4)


def from_chunks(t):
    nc, b, h, c, d = t.shape
    return t.transpose(1, 0, 3, 2, 4).reshape(b, nc * c, h, d)


def gla_recurrence(q, k, v, log_a, s0):
    c = chunk_size(q.shape[1], GLA_CHUNK)
    tri = jnp.tril(jnp.ones((c, c), dtype=bool))

    def step(S, blk):
        qc, kc, vc, ac = blk
        b = jnp.cumsum(ac, axis=2)
        inter = jnp.einsum('bhtd,bhde->bhte', qc * jnp.exp(b), S)
        diff = b[:, :, :, None, :] - b[:, :, None, :, :]
        decay = jnp.exp(jnp.where(tri[:, :, None], diff, -jnp.inf))
        scores = jnp.einsum('bhtd,bhsd,bhtsd->bhts', qc, kc, decay)
        intra = jnp.einsum('bhts,bhse->bhte', scores, vc)
        b_last = b[:, :, -1:, :]
        S_new = jnp.exp(b_last[:, :, 0, :, None]) * S + jnp.einsum('bhsd,bhse->bhde', kc * jnp.exp(b_last - b), vc)
        return S_new, inter + intra

    xs = tuple(to_chunks(t.astype(jnp.float32), c) for t in (q, k, v, log_a))
    s_fin, o = lax.scan(step, s0.astype(jnp.float32), xs)
    return from_chunks(o), s_fin


def gla_mixer(h, s0, w_in, w_a2, b_a, g_norm, w_out):
    bsz, l, _ = h.shape
    nqk = GLA_HEADS * GLA_DK
    nv = GLA_HEADS * GLA_DV
    z = h @ w_in
    q, k, v, g, a_low = jnp.split(z, [nqk, 2 * nqk, 2 * nqk + nv, 2 * nqk + 2 * nv], axis=-1)
    log_a = jax.nn.log_sigmoid((a_low @ w_a2 + b_a).astype(jnp.float32)) / GLA_TAU
    q = q.reshape(bsz, l, GLA_HEADS, GLA_DK) * GLA_DK ** -0.5
    k = k.reshape(bsz, l, GLA_HEADS, GLA_DK)
    v = v.reshape(bsz, l, GLA_HEADS, GLA_DV)
    log_a = log_a.reshape(bsz, l, GLA_HEADS, GLA_DK)
    o, s_fin = gla_recurrence(q, k, v, log_a, s0)
    o = rmsnorm(o, g_norm).reshape(bsz, l, nv)
    o = (jax.nn.silu(g.astype(jnp.float32)) * o).astype(h.dtype)
    return o @ w_out, s_fin


def retention_recurrence(q, k, v, s0):
    c = chunk_size(q.shape[1], RET_CHUNK)
    log_gamma = jnp.log1p(-jnp.exp2(-5.0 - jnp.arange(RET_HEADS, dtype=jnp.float32)))
    t = jnp.arange(c, dtype=jnp.float32)
    rel = t[:, None] - t[None, :]
    dmask = jnp.where(rel >= 0, jnp.exp(log_gamma[:, None, None] * jnp.maximum(rel, 0.0)), 0.0)
    q_decay = jnp.exp(log_gamma[:, None] * (t + 1.0))[:, :, None]
    k_decay = jnp.exp(log_gamma[:, None] * (c - 1.0 - t))[:, :, None]
    s_decay = jnp.exp(log_gamma * c)[:, None, None]

    def step(S, blk):
        qc, kc, vc = blk
        scores = jnp.einsum('bhtd,bhsd->bhts', qc, kc) * dmask
        o = jnp.einsum('bhts,bhse->bhte', scores, vc) + jnp.einsum('bhtd,bhde->bhte', qc * q_decay, S)
        S_new = s_decay * S + jnp.einsum('bhsd,bhse->bhde', kc * k_decay, vc)
        return S_new, o

    xs = tuple(to_chunks(t_.astype(jnp.float32), c) for t_ in (q, k, v))
    s_fin, o = lax.scan(step, s0.astype(jnp.float32), xs)
    return from_chunks(o), s_fin


def retention_mixer(h, pos, s0, w_in, g_norm, w_out):
    bsz, l, _ = h.shape
    nqk = RET_HEADS * RET_DK
    nv = RET_HEADS * RET_DV
    z = h @ w_in
    q, k, v, g = jnp.split(z, [nqk, 2 * nqk, 2 * nqk + nv], axis=-1)
    q = rotary(q.reshape(bsz, l, RET_HEADS, RET_DK), pos, RET_DK, RET_THETA)
    k = rotary(k.reshape(bsz, l, RET_HEADS, RET_DK), pos, RET_DK, RET_THETA) * RET_DK ** -0.5
    v = v.reshape(bsz, l, RET_HEADS, RET_DV)
    o, s_fin = retention_recurrence(q, k, v, s0)
    mu = jnp.mean(o, axis=-1, keepdims=True)
    var = jnp.mean(jnp.square(o - mu), axis=-1, keepdims=True)
    o = (o - mu) * lax.rsqrt(var + NORM_EPS) * g_norm.reshape(RET_HEADS, RET_DV).astype(jnp.float32)
    o = (jax.nn.silu(g.astype(jnp.float32)) * o.reshape(bsz, l, nv)).astype(h.dtype)
    return o @ w_out, s_fin


def swa_project(h, pos, w_in, b_in):
    bsz, l, _ = h.shape
    nq = SWA_HEADS * SWA_HD
    nkv = SWA_KV_HEADS * SWA_HD
    z = h @ w_in + b_in
    q, k, v = jnp.split(z, [nq, nq + nkv], axis=-1)
    q = rotary(q.reshape(bsz, l, SWA_HEADS, SWA_HD), pos, ROPE_DIMS, ROPE_THETA) * SWA_HD ** -0.5
    k = rotary(k.reshape(bsz, l, SWA_KV_HEADS, SWA_HD), pos, ROPE_DIMS, ROPE_THETA)
    v = v.reshape(bsz, l, SWA_KV_HEADS, SWA_HD)
    return q.reshape(bsz, l, SWA_KV_HEADS, SWA_GROUP, SWA_HD), k, v


def sink_attention(q, k, v, sinks, mask):
    s = jnp.einsum('...qkgd,...skd->...kgqs', q, k).astype(jnp.float32)
    s = jnp.where(mask, s, -jnp.inf)
    sink = sinks.astype(jnp.float32)[:, :, None, None]
    m = jnp.maximum(jnp.max(s, axis=-1, keepdims=True), sink)
    p = jnp.exp(s - m)
    denom = jnp.sum(p, axis=-1, keepdims=True) + jnp.exp(sink - m)
    return jnp.einsum('...kgqs,...skd->...qkgd', (p / denom).astype(v.dtype), v)


def swa_prompt(q, k, v, sinks):
    bsz, l = q.shape[:2]
    w = SWA_WINDOW
    nb = l // w
    qb = q.reshape(bsz, nb, w, SWA_KV_HEADS, SWA_GROUP, SWA_HD)

    def band(t):
        tp = jnp.concatenate([jnp.zeros_like(t[:, :w]), t], axis=1).reshape(bsz, nb + 1, w, SWA_KV_HEADS, SWA_HD)
        return jnp.concatenate([tp[:, :-1], tp[:, 1:]], axis=2)

    qi = jnp.arange(w)[:, None]
    kj = jnp.arange(2 * w)[None, :]
    rel = qi + w - kj
    blk = jnp.arange(nb)[:, None, None]
    mask = (rel >= 0) & (rel <= w) & ((blk > 0) | (kj >= w))
    o = sink_attention(qb, band(k), band(v), sinks, mask[:, None, None])
    return o.reshape(bsz, l, SWA_HEADS * SWA_HD)


def swa_sample(q, k, v, ck, cv, sinks):
    bsz, l = q.shape[:2]
    w = ck.shape[1]
    kk = jnp.concatenate([ck.astype(k.dtype), k], axis=1)
    vv = jnp.concatenate([cv.astype(v.dtype), v], axis=1)
    rel = (w + jnp.arange(l))[:, None] - jnp.arange(w + l)[None, :]
    mask = (rel >= 0) & (rel <= SWA_WINDOW)
    o = sink_attention(q, kk, vv, sinks, mask)
    return o.reshape(bsz, l, SWA_HEADS * SWA_HD)


def trunk(x, p, pos, gla_s0, ret_s0, swa_k0, swa_v0, *, norm_mix, norm_mlp, norm_ple, norm_final,
          gla_w_in, gla_w_a2, gla_b_a, gla_norm, gla_w_out, ret_w_in, ret_norm, ret_w_out,
          swa_w_in, swa_b_in, swa_sinks, swa_w_out, swa_b_out, mlp_w_up, mlp_w_down,
          ple_w_gate, ple_w_proj):
    gla_new, ret_new, swa_k_new, swa_v_new = [], [], [], []
    for i in range(DEPTH):
        j = i // N_MIXERS
        h = rmsnorm(x, norm_mix[i])
        if i % N_MIXERS == 0:
            mix, s = gla_mixer(h, gla_s0[j], gla_w_in[j], gla_w_a2[j], gla_b_a[j], gla_norm[j], gla_w_out[j])
            gla_new.append(s)
        elif i % N_MIXERS == 1:
            mix, s = retention_mixer(h, pos, ret_s0[j], ret_w_in[j], ret_norm[j], ret_w_out[j])
            ret_new.append(s)
        else:
            q, k, v = swa_project(h, pos, swa_w_in[j], swa_b_in[j])
            sinks = swa_sinks[j].reshape(SWA_KV_HEADS, SWA_GROUP)
            if swa_k0 is None:
                o = swa_prompt(q, k, v, sinks)
                swa_k_new.append(k[:, -SWA_WINDOW:])
                swa_v_new.append(v[:, -SWA_WINDOW:])
            else:
                o = swa_sample(q, k, v, swa_k0[j], swa_v0[j], sinks)
                swa_k_new.append(k)
                swa_v_new.append(v)
            mix = o @ swa_w_out[j] + swa_b_out[j]
        x = x + mix
        u = rmsnorm(x, norm_mlp[i]) @ mlp_w_up[i]
        x = x + jnp.square(jax.nn.relu(u)) @ mlp_w_down[i]
        gate = jax.nn.sigmoid(rmsnorm(x, norm_ple[i]) @ ple_w_gate[i])
        x = x + gate * (p[i] @ ple_w_proj[i])
    y = rmsnorm(x, norm_final)
    return y, jnp.stack(gla_new), jnp.stack(ret_new), jnp.stack(swa_k_new), jnp.stack(swa_v_new)


def setup_inputs(seed: int = 0) -> dict:
    key = jax.random.key(seed)
    ks = iter(jax.random.split(key, 32))

    def nrm(shape, scale):
        return jax.random.normal(next(ks), shape, jnp.float32) * scale

    def gain(shape):
        return 1.0 + nrm(shape, 0.02)

    n_gla_in = 2 * GLA_HEADS * GLA_DK + 2 * GLA_HEADS * GLA_DV + GLA_RANK
    n_ret_in = 2 * RET_HEADS * RET_DK + 2 * RET_HEADS * RET_DV
    n_swa_in = (SWA_HEADS + 2 * SWA_KV_HEADS) * SWA_HD
    return {
        'x_prompt': nrm((BATCH, SEQ, D_MODEL), 1.0),
        'x_sample': nrm((DEC_BATCH, DEC_SEQ, D_MODEL), 1.0),
        'state_gla': nrm((N_LAYERS_A, DEC_BATCH, GLA_HEADS, GLA_DK, GLA_DV), 1.0),
        'state_ret': nrm((N_LAYERS_B, DEC_BATCH, RET_HEADS, RET_DK, RET_DV), 1.0),
        'cache_swa_k': nrm((N_LAYERS_C, DEC_BATCH, SWA_WINDOW, SWA_KV_HEADS, SWA_HD), 1.0),
        'cache_swa_v': nrm((N_LAYERS_C, DEC_BATCH, SWA_WINDOW, SWA_KV_HEADS, SWA_HD), 1.0),
        'p_prompt': nrm((DEPTH, BATCH, SEQ, PLE_DIM), 1.0),
        'p_sample': nrm((DEPTH, DEC_BATCH, DEC_SEQ, PLE_DIM), 1.0),
        'norm_mix': gain((DEPTH, D_MODEL)),
        'norm_mlp': gain((DEPTH, D_MODEL)),
        'norm_ple': gain((DEPTH, D_MODEL)),
        'norm_final': gain((D_MODEL,)),
        'gla_w_in': nrm((N_LAYERS_A, D_MODEL, n_gla_in), D_MODEL ** -0.5),
        'gla_w_a2': nrm((N_LAYERS_A, GLA_RANK, GLA_HEADS * GLA_DK), GLA_RANK ** -0.5),
        'gla_b_a': nrm((N_LAYERS_A, GLA_HEADS * GLA_DK), 0.1),
        'gla_norm': gain((N_LAYERS_A, GLA_DV)),
        'gla_w_out': nrm((N_LAYERS_A, GLA_HEADS * GLA_DV, D_MODEL), (GLA_HEADS * GLA_DV) ** -0.5),
        'ret_w_in': nrm((N_LAYERS_B, D_MODEL, n_ret_in), D_MODEL ** -0.5),
        'ret_norm': gain((N_LAYERS_B, RET_HEADS * RET_DV)),
        'ret_w_out': nrm((N_LAYERS_B, RET_HEADS * RET_DV, D_MODEL), (RET_HEADS * RET_DV) ** -0.5),
        'swa_w_in': nrm((N_LAYERS_C, D_MODEL, n_swa_in), D_MODEL ** -0.5),
        'swa_b_in': nrm((N_LAYERS_C, n_swa_in), 0.02),
        'swa_sinks': nrm((N_LAYERS_C, SWA_HEADS), 0.5),
        'swa_w_out': nrm((N_LAYERS_C, SWA_HEADS * SWA_HD, D_MODEL), (SWA_HEADS * SWA_HD) ** -0.5),
        'swa_b_out': nrm((N_LAYERS_C, D_MODEL), 0.02),
        'mlp_w_up': nrm((DEPTH, D_MODEL, D_FF), D_MODEL ** -0.5),
        'mlp_w_down': nrm((DEPTH, D_FF, D_MODEL), D_FF ** -0.5),
        'ple_w_gate': nrm((DEPTH, D_MODEL, D_MODEL), D_MODEL ** -0.5),
        'ple_w_proj': nrm((DEPTH, PLE_DIM, D_MODEL), PLE_DIM ** -0.5),
    }


def reference(x_prompt, x_sample, state_gla, state_ret, cache_swa_k, cache_swa_v, p_prompt, p_sample,
              norm_mix, norm_mlp, norm_ple, norm_final, gla_w_in, gla_w_a2, gla_b_a, gla_norm, gla_w_out,
              ret_w_in, ret_norm, ret_w_out, swa_w_in, swa_b_in, swa_sinks, swa_w_out, swa_b_out,
              mlp_w_up, mlp_w_down, ple_w_gate, ple_w_proj):
    run = functools.partial(
        trunk, norm_mix=norm_mix, norm_mlp=norm_mlp, norm_ple=norm_ple, norm_final=norm_final,
        gla_w_in=gla_w_in, gla_w_a2=gla_w_a2, gla_b_a=gla_b_a, gla_norm=gla_norm, gla_w_out=gla_w_out,
        ret_w_in=ret_w_in, ret_norm=ret_norm, ret_w_out=ret_w_out,
        swa_w_in=swa_w_in, swa_b_in=swa_b_in, swa_sinks=swa_sinks, swa_w_out=swa_w_out, swa_b_out=swa_b_out,
        mlp_w_up=mlp_w_up, mlp_w_down=mlp_w_down, ple_w_gate=ple_w_gate, ple_w_proj=ple_w_proj)

    bp, lp = x_prompt.shape[0], x_prompt.shape[1]
    gla_zero = jnp.zeros((N_LAYERS_A, bp, GLA_HEADS, GLA_DK, GLA_DV), jnp.float32)
    ret_zero = jnp.zeros((N_LAYERS_B, bp, RET_HEADS, RET_DK, RET_DV), jnp.float32)
    pos_prompt = jnp.arange(lp, dtype=jnp.int32)
    y_prompt, sg_p, sr_p, ck_p, cv_p = run(x_prompt, p_prompt, pos_prompt, gla_zero, ret_zero, None, None)

    pos_sample = PAST_LEN + jnp.arange(x_sample.shape[1], dtype=jnp.int32)
    y_sample, sg_s, sr_s, ck_s, cv_s = run(x_sample, p_sample, pos_sample, state_gla, state_ret, cache_swa_k, cache_swa_v)

    return (y_prompt, y_sample, sg_p, sg_s, sr_p, sr_s, ck_p, cv_p, ck_s, cv_s)
```

```python
import functools

import jax
import jax.numpy as jnp
from jax import lax
from jax.experimental import pallas as pl
from jax.experimental.pallas import tpu as pltpu

F32 = jnp.float32
BF16 = jnp.bfloat16

VMEM_LIMIT_BYTES = 56 * 1024 * 1024
LANES = 128
SUBLANES = 8

NORM_EPS = 1e-6
PAST_LEN = 8192
N_MIXERS = 3

GLA_HEADS = 8
GLA_TAU = 16.0
GLA_CHUNK = 64
GLA_SUB = 16
GLA_ROWS = 256

RET_HEADS = 16
RET_THETA = 10000.0
RET_CHUNK = 256

SWA_HEADS = 64
SWA_KV_HEADS = 8
SWA_WINDOW = 128
ROPE_THETA = 500000.0

ROW_BLOCK = 1664
COL_BLOCK = 256
DEEP_COL_BLOCK = 512
DEEP_K_BLOCK = 2048
NORM_ROWS = 416


def _params(*sem):
    return pltpu.CompilerParams(dimension_semantics=sem, vmem_limit_bytes=VMEM_LIMIT_BYTES)


def _row_block(m, target):
    if m <= target:
        return m
    for d in range(target, SUBLANES - 1, -1):
        if m % d == 0 and d % SUBLANES == 0:
            return d
    raise ValueError(f"no row block for {m}")


def _dot(a, b):
    return jnp.dot(a, b, preferred_element_type=F32)


def _dot_nt(a, b):
    return lax.dot_general(a, b, (((1,), (1,)), ((), ())), preferred_element_type=F32)


def _dot_tn(a, b):
    return lax.dot_general(a, b, (((0,), (0,)), ((), ())), preferred_element_type=F32)


def _silu(x):
    return x * jax.nn.sigmoid(x)


def _rmsnorm_kernel(x_ref, g_ref, o_ref):
    x = x_ref[...]
    ms = jnp.mean(x * x, axis=-1, keepdims=True)
    o_ref[...] = (x * lax.rsqrt(ms + NORM_EPS) * g_ref[...]).astype(o_ref.dtype)


def _rmsnorm(x, g, out_dtype):
    m, d = x.shape
    bm = _row_block(m, NORM_ROWS)
    return pl.pallas_call(
        _rmsnorm_kernel,
        grid=(m // bm,),
        in_specs=[pl.BlockSpec((bm, d), lambda i: (i, 0)), pl.BlockSpec((1, d), lambda i: (0, 0))],
        out_specs=pl.BlockSpec((bm, d), lambda i: (i, 0)),
        out_shape=jax.ShapeDtypeStruct((m, d), out_dtype),
        compiler_params=_params("parallel"),
        name="rmsnorm",
    )(x, g.reshape(1, d))


def _matmul_kernel(*refs, nk, has_bias, has_resid, has_gate, act):
    it = iter(refs)
    x_ref = next(it)
    w_ref = next(it)
    b_ref = next(it) if has_bias else None
    r_ref = next(it) if has_resid else None
    p_ref = next(it) if has_gate else None
    wp_ref = next(it) if has_gate else None
    o_ref = next(it)
    acc_ref = next(it) if nk > 1 else None

    part = _dot(x_ref[...], w_ref[...].astype(BF16))

    def finish(acc):
        if has_bias:
            acc = acc + b_ref[...]
        if act == "relu2":
            r = jnp.maximum(acc, 0.0)
            acc = r * r
        if has_gate:
            proj = _dot(p_ref[...].astype(BF16), wp_ref[...].astype(BF16))
            acc = jax.nn.sigmoid(acc) * proj
        if has_resid:
            acc = r_ref[...] + acc
        o_ref[...] = acc.astype(o_ref.dtype)

    if nk == 1:
        finish(part)
    else:
        k = pl.program_id(2)

        @pl.when(k == 0)
        def _():
            acc_ref[...] = part

        @pl.when(k > 0)
        def _():
            acc_ref[...] += part

        @pl.when(k == nk - 1)
        def _():
            finish(acc_ref[...])


def _matmul(x, w, layer, *, n=None, bias=None, resid=None, gate=None, act=None, out_dtype, name):
    m, kdim = x.shape
    n = w.shape[2] if n is None else n
    bm = _row_block(m, ROW_BLOCK)
    if kdim <= 4096:
        bk, bn = kdim, min(n, COL_BLOCK)
    else:
        bk, bn = DEEP_K_BLOCK, min(n, DEEP_COL_BLOCK)
    assert n % bn == 0 and kdim % bk == 0 and m % bm == 0
    nk = kdim // bk
    grid = (m // bm, n // bn, nk)

    in_specs = [
        pl.BlockSpec((bm, bk), lambda i, j, k: (i, k)),
        pl.BlockSpec((None, bk, bn), lambda i, j, k: (layer, k, j)),
    ]
    args = [x, w]
    if bias is not None:
        in_specs.append(pl.BlockSpec((None, 1, bn), lambda i, j, k: (layer, 0, j)))
        args.append(bias.reshape(bias.shape[0], 1, bias.shape[1]))
    if resid is not None:
        in_specs.append(pl.BlockSpec((bm, bn), lambda i, j, k: (i, j)))
        args.append(resid)
    if gate is not None:
        p, wp = gate
        kp = p.shape[1]
        in_specs.append(pl.BlockSpec((bm, kp), lambda i, j, k: (i, 0)))
        in_specs.append(pl.BlockSpec((None, kp, bn), lambda i, j, k: (layer, 0, j)))
        args += [p, wp]
    scratch = [pltpu.VMEM((bm, bn), F32)] if nk > 1 else []
    kern = functools.partial(_matmul_kernel, nk=nk, has_bias=bias is not None, has_resid=resid is not None,
                             has_gate=gate is not None, act=act)
    return pl.pallas_call(
        kern,
        grid=grid,
        in_specs=in_specs,
        out_specs=pl.BlockSpec((bm, bn), lambda i, j, k: (i, j)),
        out_shape=jax.ShapeDtypeStruct((m, n), out_dtype),
        scratch_shapes=scratch,
        compiler_params=_params("parallel", "parallel", "arbitrary"),
        name=name,
    )(*args)


def _split3(x):
    hi = x.astype(BF16)
    r1 = x - hi.astype(F32)
    mid = r1.astype(BF16)
    lo = (r1 - mid.astype(F32)).astype(BF16)
    return hi, mid, lo


def _gla_kernel(*refs, rows, chunk, sub, valid, has_s0):
    it = iter(refs)
    q_ref, k_ref, v_ref, g_ref, al_ref, wa_ref, ba_ref, gn_ref = (next(it) for _ in range(8))
    s0_ref = next(it) if has_s0 else None
    o_ref, sfin_ref, s_ref = next(it), next(it), next(it)

    dk = q_ref.shape[1]
    dv = v_ref.shape[1]
    t = pl.program_id(2)

    @pl.when(t == 0)
    def _():
        if has_s0:
            s_ref[...] = s0_ref[...]
        else:
            s_ref[...] = jnp.zeros_like(s_ref)

    row_c = lax.broadcasted_iota(jnp.int32, (chunk, chunk), 0)
    col_c = lax.broadcasted_iota(jnp.int32, (chunk, chunk), 1)
    tri = (row_c >= col_c).astype(BF16)
    ones = jnp.ones((chunk, LANES), BF16)
    lane = lax.broadcasted_iota(jnp.int32, (sub, chunk), 1)
    subrow = lax.broadcasted_iota(jnp.int32, (sub, dk), 0)
    chunkrow = lax.broadcasted_iota(jnp.int32, (chunk, dk), 0)

    def chunk_body(ci, carry):
        r0 = pl.multiple_of(ci * chunk, chunk)
        rs = pl.ds(r0, chunk)
        q = q_ref[rs, :].astype(F32) * (dk ** -0.5)
        k = k_ref[rs, :].astype(F32)
        v = v_ref[rs, :]
        x = _dot(al_ref[rs, :], wa_ref[...]) + ba_ref[...]
        la = (jnp.minimum(x, 0.0) - jnp.log1p(jnp.exp(-jnp.abs(x)))) * (1.0 / GLA_TAU)
        if valid < chunk:
            la = jnp.where(chunkrow < valid, la, 0.0)
        parts = _split3(la)
        b = sum(_dot(tri, p) for p in parts)
        b_tot = sum(_dot_tn(p, ones) for p in parts)

        s_old = s_ref[...]
        o = _dot((q * jnp.exp(b)).astype(BF16), s_old.astype(BF16))

        blocks = []
        for i in range(chunk // sub):
            lo_r = i * sub
            b_r = b[lo_r:lo_r + sub, :]
            q_r = q[lo_r:lo_r + sub, :]
            acc = jnp.zeros((sub, chunk), F32)
            if i > 0:
                ref = b[lo_r - 1:lo_r, :]
                qt = (q_r * jnp.exp(b_r - ref)).astype(BF16)
                kt = (k * jnp.exp(jnp.minimum(ref - b, 0.0))).astype(BF16)
                acc = jnp.where(lane < lo_r, _dot_nt(qt, kt), 0.0)
            for s in range(sub):
                row = lo_r + s
                e = jnp.exp(jnp.where(subrow >= s, b_r - b[row:row + 1, :], -jnp.inf))
                col = jnp.sum(q_r * k[row:row + 1, :] * e, axis=-1, keepdims=True)
                acc = acc + jnp.where(lane == row, col, 0.0)
            blocks.append(acc)
        scores = blocks[0] if len(blocks) == 1 else jnp.concatenate(blocks, axis=0)
        o = o + _dot(scores.astype(BF16), v)

        kd = (k * jnp.exp(b[chunk - 1:chunk, :] - b)).astype(BF16)
        dec = jnp.exp(b_tot)
        dec = jnp.concatenate([dec] * (dv // LANES), axis=1)
        s_ref[...] = s_old * dec + _dot_tn(kd, v)

        ms = jnp.mean(o * o, axis=-1, keepdims=True)
        on = o * lax.rsqrt(ms + NORM_EPS) * gn_ref[...]
        o_ref[rs, :] = (_silu(g_ref[rs, :].astype(F32)) * on).astype(o_ref.dtype)
        return carry

    lax.fori_loop(0, rows // chunk, chunk_body, 0)

    @pl.when(t == pl.num_programs(2) - 1)
    def _():
        sfin_ref[...] = s_ref[...]


def _gla_recurrence(z, alow, w_a2, b_a, g_norm, layer, s0, *, batch, length, rows, chunk, sub, valid):
    h = GLA_HEADS
    dk = w_a2.shape[2] // h
    dv = g_norm.shape[1]
    nt = length // rows
    assert length % rows == 0 and rows % chunk == 0 and chunk % sub == 0
    kcol, vcol, gcol = h, (2 * h * dk) // dv, (2 * h * dk) // dv + h

    in_specs = [
        pl.BlockSpec((rows, dk), lambda b, hh, t: (b * nt + t, hh)),
        pl.BlockSpec((rows, dk), lambda b, hh, t: (b * nt + t, kcol + hh)),
        pl.BlockSpec((rows, dv), lambda b, hh, t: (b * nt + t, vcol + hh)),
        pl.BlockSpec((rows, dv), lambda b, hh, t: (b * nt + t, gcol + hh)),
        pl.BlockSpec((rows, alow.shape[1]), lambda b, hh, t: (b * nt + t, 0)),
        pl.BlockSpec((None, w_a2.shape[1], dk), lambda b, hh, t: (layer, 0, hh)),
        pl.BlockSpec((None, 1, dk), lambda b, hh, t: (layer, 0, hh)),
        pl.BlockSpec((None, 1, dv), lambda b, hh, t: (layer, 0, 0)),
    ]
    args = [z, z, z, z, alow, w_a2, b_a.reshape(b_a.shape[0], 1, -1), g_norm.reshape(g_norm.shape[0], 1, -1)]
    if s0 is not None:
        in_specs.append(pl.BlockSpec((None, None, None, dk, dv), lambda b, hh, t: (layer, b, hh, 0, 0)))
        args.append(s0)
    kern = functools.partial(_gla_kernel, rows=rows, chunk=chunk, sub=sub, valid=valid, has_s0=s0 is not None)
    return pl.pallas_call(
        kern,
        grid=(batch, h, nt),
        in_specs=in_specs,
        out_specs=[
            pl.BlockSpec((rows, dv), lambda b, hh, t: (b * nt + t, hh)),
            pl.BlockSpec((None, None, dk, dv), lambda b, hh, t: (b, hh, 0, 0)),
        ],
        out_shape=[
            jax.ShapeDtypeStruct((batch * length, h * dv), BF16),
            jax.ShapeDtypeStruct((batch, h, dk, dv), F32),
        ],
        scratch_shapes=[pltpu.VMEM((dk, dv), F32)],
        compiler_params=_params("parallel", "parallel", "arbitrary"),
        name="gla_recurrence",
    )(*args)


def _ret_kernel(*refs, valid, has_s0):
    it = iter(refs)
    q_ref, k_ref, v_ref, g_ref, cos_ref, sin_ref, lg_ref, gn_ref = (next(it) for _ in range(8))
    s0_ref = next(it) if has_s0 else None
    o_ref, sfin_ref, s_ref = next(it), next(it), next(it)

    c, dk = q_ref.shape
    dv = v_ref.shape[1]
    half = dk // 2
    t = pl.program_id(2)

    @pl.when(t == 0)
    def _():
        if has_s0:
            s_ref[...] = s0_ref[...]
        else:
            s_ref[...] = jnp.zeros_like(s_ref)

    cos = cos_ref[...]
    sin = sin_ref[...]

    def rot(x):
        x1, x2 = x[:, :half], x[:, half:]
        return jnp.concatenate([x1 * cos - x2 * sin, x2 * cos + x1 * sin], axis=1)

    qr = rot(q_ref[...].astype(F32))
    kr = rot(k_ref[...].astype(F32)) * (dk ** -0.5)
    v = v_ref[...]

    lg = lg_ref[...]
    ti = lax.broadcasted_iota(jnp.int32, (c, half), 0).astype(F32)
    q_dec = jnp.exp(lg * (ti + 1.0))
    k_dec = jnp.exp(lg * (valid - 1.0 - ti))
    q_dec = jnp.concatenate([q_dec] * (dk // half), axis=1)
    k_dec = jnp.concatenate([k_dec] * (dk // half), axis=1)
    rel = (lax.broadcasted_iota(jnp.int32, (c, c), 0) - lax.broadcasted_iota(jnp.int32, (c, c), 1)).astype(F32)
    dmask = jnp.where(rel >= 0.0, jnp.exp(lg[:, :1] * jnp.maximum(rel, 0.0)), 0.0)

    scores = _dot_nt(qr.astype(BF16), kr.astype(BF16)) * dmask
    s_old = s_ref[...]
    o = _dot(scores.astype(BF16), v) + _dot((qr * q_dec).astype(BF16), s_old.astype(BF16))
    s_dec = jnp.concatenate([jnp.exp(lg * float(valid))] * (dv // LANES), axis=1)
    s_ref[...] = s_old * s_dec + _dot_tn((kr * k_dec).astype(BF16), v)

    mu = jnp.mean(o, axis=-1, keepdims=True)
    d = o - mu
    var = jnp.mean(d * d, axis=-1, keepdims=True)
    on = d * lax.rsqrt(var + NORM_EPS) * gn_ref[...]
    o_ref[...] = (_silu(g_ref[...].astype(F32)) * on).astype(o_ref.dtype)

    @pl.when(t == pl.num_programs(2) - 1)
    def _():
        sfin_ref[...] = s_ref[...]


def _ret_recurrence(z, pos, g_norm, layer, s0, *, batch, length, chunk, valid):
    h = RET_HEADS
    dv = g_norm.shape[1] // h
    dk = (z.shape[1] - 2 * h * dv) // (2 * h)
    half = dk // 2
    nt = length // chunk
    assert length % chunk == 0 and half == LANES
    kcol, vcol, gcol = h, (2 * h * dk) // dv, (2 * h * dk) // dv + h

    inv_freq = RET_THETA ** (-jnp.arange(half, dtype=F32) / half)
    ang = pos.astype(F32)[:, None] * inv_freq[None, :]
    cos, sin = jnp.cos(ang), jnp.sin(ang)
    log_gamma = jnp.log1p(-jnp.exp2(-5.0 - jnp.arange(h, dtype=F32)))
    lg = jnp.broadcast_to(log_gamma[:, None, None], (h, 1, LANES))

    in_specs = [
        pl.BlockSpec((chunk, dk), lambda hh, b, t: (b * nt + t, hh)),
        pl.BlockSpec((chunk, dk), lambda hh, b, t: (b * nt + t, kcol + hh)),
        pl.BlockSpec((chunk, dv), lambda hh, b, t: (b * nt + t, vcol + hh)),
        pl.BlockSpec((chunk, dv), lambda hh, b, t: (b * nt + t, gcol + hh)),
        pl.BlockSpec((chunk, half), lambda hh, b, t: (t, 0)),
        pl.BlockSpec((chunk, half), lambda hh, b, t: (t, 0)),
        pl.BlockSpec((None, 1, LANES), lambda hh, b, t: (hh, 0, 0)),
        pl.BlockSpec((None, 1, dv), lambda hh, b, t: (layer, 0, hh)),
    ]
    args = [z, z, z, z, cos, sin, lg, g_norm.reshape(g_norm.shape[0], 1, -1)]
    if s0 is not None:
        in_specs.append(pl.BlockSpec((None, None, None, dk, dv), lambda hh, b, t: (layer, b, hh, 0, 0)))
        args.append(s0)
    kern = functools.partial(_ret_kernel, valid=valid, has_s0=s0 is not None)
    return pl.pallas_call(
        kern,
        grid=(h, batch, nt),
        in_specs=in_specs,
        out_specs=[
            pl.BlockSpec((chunk, dv), lambda hh, b, t: (b * nt + t, hh)),
            pl.BlockSpec((None, None, dk, dv), lambda hh, b, t: (b, hh, 0, 0)),
        ],
        out_shape=[
            jax.ShapeDtypeStruct((batch * length, h * dv), BF16),
            jax.ShapeDtypeStruct((batch, h, dk, dv), F32),
        ],
        scratch_shapes=[pltpu.VMEM((dk, dv), F32)],
        compiler_params=_params("parallel", "parallel", "arbitrary"),
        name="retention_recurrence",
    )(*args)


def _rope_tables(pos, hd, n_rot):
    half = n_rot // 2
    inv_freq = ROPE_THETA ** (-jnp.arange(half, dtype=F32) / half)
    ang = pos.astype(F32)[:, None] * inv_freq[None, :]
    cos, sin = jnp.cos(ang), jnp.sin(ang)
    n = pos.shape[0]
    ones = jnp.ones((n, hd - n_rot), F32)
    zeros = jnp.zeros((n, hd - n_rot), F32)
    zh = jnp.zeros((n, half), F32)
    a = jnp.concatenate([cos, cos, ones], axis=1)
    b = jnp.concatenate([-sin, zh, zeros], axis=1)
    c = jnp.concatenate([zh, sin, zeros], axis=1)
    rep = LANES // hd
    return jnp.tile(a, (1, rep)), jnp.tile(b, (1, rep)), jnp.tile(c, (1, rep))


def _rope128(x, a, b, c, half):
    return x * a + pltpu.roll(x, LANES - half, 1) * b + pltpu.roll(x, half, 1) * c


def _softmax_sink(parts, sink):
    m = sink
    for s in parts:
        m = jnp.maximum(m, jnp.max(s, axis=-1, keepdims=True))
    ps = [jnp.exp(s - m) for s in parts]
    denom = jnp.exp(sink - m)
    for p in ps:
        denom = denom + jnp.sum(p, axis=-1, keepdims=True)
    return [p / denom for p in ps]


def _swa_prompt_kernel(sinks_ref, q_ref, k_ref, v_ref, a_ref, b_ref, c_ref, o_ref, krot_ref,
                       qrot_ref, kcat_ref, vcat_ref, *, hd, half):
    w = q_ref.shape[0]
    nq = q_ref.shape[1]
    nkv = k_ref.shape[1]
    n_heads, n_kv = nq // hd, nkv // hd
    group = n_heads // n_kv
    i = pl.program_id(1)

    @pl.when(i == 0)
    def _():
        kcat_ref[0:w, :] = jnp.zeros((w, nkv), BF16)
        vcat_ref[0:w, :] = jnp.zeros((w, nkv), BF16)

    a, b, c = a_ref[...], b_ref[...], c_ref[...]
    scale = hd ** -0.5
    for m in range(nq // LANES):
        sl = slice(m * LANES, (m + 1) * LANES)
        qrot_ref[:, sl] = (_rope128(q_ref[:, sl], a, b, c, half) * scale).astype(BF16)
    for m in range(nkv // LANES):
        sl = slice(m * LANES, (m + 1) * LANES)
        kr = _rope128(k_ref[:, sl], a, b, c, half)
        krot_ref[:, sl] = kr
        kcat_ref[w:2 * w, sl] = kr.astype(BF16)
    vcat_ref[w:2 * w, :] = v_ref[...].astype(BF16)

    qi = lax.broadcasted_iota(jnp.int32, (w, 2 * w), 0)
    kj = lax.broadcasted_iota(jnp.int32, (w, 2 * w), 1)
    rel = qi + w - kj
    first_key = jnp.where(i > 0, 0, w)
    mask = (rel >= 0) & (rel <= w) & (kj >= first_key)

    for kv in range(n_kv):
        ksl = slice(kv * hd, (kv + 1) * hd)
        kc = kcat_ref[:, ksl]
        vc = vcat_ref[:, ksl]
        for gq in range(group):
            hq = kv * group + gq
            qsl = slice(hq * hd, (hq + 1) * hd)
            s = jnp.where(mask, _dot_nt(qrot_ref[:, qsl], kc), -jnp.inf)
            (p,) = _softmax_sink([s], sinks_ref[hq])
            o_ref[:, qsl] = _dot(p.astype(BF16), vc).astype(o_ref.dtype)

    kcat_ref[0:w, :] = kcat_ref[w:2 * w, :]
    vcat_ref[0:w, :] = vcat_ref[w:2 * w, :]


def _swa_prompt(z, sinks, *, batch, length, hd, n_rot):
    w = SWA_WINDOW
    nq, nkv = SWA_HEADS * hd, SWA_KV_HEADS * hd
    nb = length // w
    assert length % w == 0 and nq % nkv == 0
    a, b, c = _rope_tables(jnp.arange(length, dtype=jnp.int32), hd, n_rot)
    kcol = nq // nkv
    tab = pl.BlockSpec((w, LANES), lambda bb, i: (i, 0))
    kern = functools.partial(_swa_prompt_kernel, hd=hd, half=n_rot // 2)
    return pl.pallas_call(
        kern,
        grid=(batch, nb),
        in_specs=[
            pl.BlockSpec(memory_space=pltpu.SMEM),
            pl.BlockSpec((w, nq), lambda bb, i: (bb * nb + i, 0)),
            pl.BlockSpec((w, nkv), lambda bb, i: (bb * nb + i, kcol)),
            pl.BlockSpec((w, nkv), lambda bb, i: (bb * nb + i, kcol + 1)),
            tab, tab, tab,
        ],
        out_specs=[
            pl.BlockSpec((w, nq), lambda bb, i: (bb * nb + i, 0)),
            pl.BlockSpec((None, w, nkv), lambda bb, i: (bb, 0, 0)),
        ],
        out_shape=[
            jax.ShapeDtypeStruct((batch * length, nq), BF16),
            jax.ShapeDtypeStruct((batch, w, nkv), F32),
        ],
        scratch_shapes=[pltpu.VMEM((w, nq), BF16), pltpu.VMEM((2 * w, nkv), BF16), pltpu.VMEM((2 * w, nkv), BF16)],
        compiler_params=_params("parallel", "arbitrary"),
        name="swa_prompt",
    )(sinks, z, z, z, a, b, c)


def _swa_sample_kernel(sinks_ref, q_ref, k_ref, v_ref, ck_ref, cv_ref, a_ref, b_ref, c_ref, o_ref, krot_ref,
                       qrot_ref, knew_ref, *, hd, half, valid):
    rows = q_ref.shape[0]
    w = ck_ref.shape[0]
    nq = q_ref.shape[1]
    nkv = k_ref.shape[1]
    n_heads, n_kv = nq // hd, nkv // hd
    group = n_heads // n_kv

    a, b, c = a_ref[...], b_ref[...], c_ref[...]
    scale = hd ** -0.5
    for m in range(nq // LANES):
        sl = slice(m * LANES, (m + 1) * LANES)
        qrot_ref[:, sl] = (_rope128(q_ref[:, sl], a, b, c, half) * scale).astype(BF16)
    for m in range(nkv // LANES):
        sl = slice(m * LANES, (m + 1) * LANES)
        kr = _rope128(k_ref[:, sl], a, b, c, half)
        krot_ref[:, sl] = kr
        knew_ref[:, sl] = kr.astype(BF16)

    lc = lax.broadcasted_iota(jnp.int32, (rows, w), 0)
    jc = lax.broadcasted_iota(jnp.int32, (rows, w), 1)
    mask_c = jc >= lc
    ln = lax.broadcasted_iota(jnp.int32, (rows, rows), 0)
    jn = lax.broadcasted_iota(jnp.int32, (rows, rows), 1)
    mask_n = (jn <= ln) & (jn < valid)

    for kv in range(n_kv):
        ksl = slice(kv * hd, (kv + 1) * hd)
        kc = ck_ref[:, ksl].astype(BF16)
        vc = cv_ref[:, ksl].astype(BF16)
        kn = knew_ref[:, ksl]
        vn = v_ref[:, ksl].astype(BF16)
        for gq in range(group):
            hq = kv * group + gq
            qsl = slice(hq * hd, (hq + 1) * hd)
            qh = qrot_ref[:, qsl]
            sc = jnp.where(mask_c, _dot_nt(qh, kc), -jnp.inf)
            sn = jnp.where(mask_n, _dot_nt(qh, kn), -jnp.inf)
            pc, pn = _softmax_sink([sc, sn], sinks_ref[hq])
            o_ref[:, qsl] = (_dot(pc.astype(BF16), vc) + _dot(pn.astype(BF16), vn)).astype(o_ref.dtype)


def _swa_sample(z, ck, cv, sinks, pos, *, batch, rows, valid, hd, n_rot):
    nq, nkv = SWA_HEADS * hd, SWA_KV_HEADS * hd
    w = ck.shape[1]
    a, b, c = _rope_tables(pos, hd, n_rot)
    kcol = nq // nkv
    tab = pl.BlockSpec((rows, LANES), lambda bb: (0, 0))
    kern = functools.partial(_swa_sample_kernel, hd=hd, half=n_rot // 2, valid=valid)
    return pl.pallas_call(
        kern,
        grid=(batch,),
        in_specs=[
            pl.BlockSpec(memory_space=pltpu.SMEM),
            pl.BlockSpec((None, rows, nq), lambda bb: (bb, 0, 0)),
            pl.BlockSpec((None, rows, nkv), lambda bb: (bb, 0, kcol)),
            pl.BlockSpec((None, rows, nkv), lambda bb: (bb, 0, kcol + 1)),
            pl.BlockSpec((None, w, nkv), lambda bb: (bb, 0, 0)),
            pl.BlockSpec((None, w, nkv), lambda bb: (bb, 0, 0)),
            tab, tab, tab,
        ],
        out_specs=[
            pl.BlockSpec((None, rows, nq), lambda bb: (bb, 0, 0)),
            pl.BlockSpec((None, rows, nkv), lambda bb: (bb, 0, 0)),
        ],
        out_shape=[
            jax.ShapeDtypeStruct((batch, rows, nq), BF16),
            jax.ShapeDtypeStruct((batch, rows, nkv), F32),
        ],
        scratch_shapes=[pltpu.VMEM((rows, nq), BF16), pltpu.VMEM((rows, nkv), BF16)],
        compiler_params=_params("parallel"),
        name="swa_sample",
    )(sinks, z, z, z, ck, cv, a, b, c)


def _pad_rows(x, batch, length, rows):
    n = x.shape[1]
    x = x.reshape(batch, length, n)
    x = jnp.pad(x, ((0, 0), (0, rows - length), (0, 0)))
    return x.reshape(batch * rows, n)


def _unpad_rows(x, batch, length, rows):
    n = x.shape[1]
    return x.reshape(batch, rows, n)[:, :length].reshape(batch * length, n)


def kernel(x_prompt, x_sample, state_gla, state_ret, cache_swa_k, cache_swa_v, p_prompt, p_sample, norm_mix, norm_mlp, norm_ple, norm_final, gla_w_in, gla_w_a2, gla_b_a, gla_norm, gla_w_out, ret_w_in, ret_norm, ret_w_out, swa_w_in, swa_b_in, swa_sinks, swa_w_out, swa_b_out, mlp_w_up, mlp_w_down, ple_w_gate, ple_w_proj):
    bp, lp, d = x_prompt.shape
    bs, ls, _ = x_sample.shape
    depth = norm_mix.shape[0]
    mp, ms = bp * lp, bs * ls
    ls_pad = SUBLANES
    pos_p = jnp.arange(lp, dtype=jnp.int32)
    pos_s = PAST_LEN + jnp.arange(ls_pad, dtype=jnp.int32)

    x = jnp.concatenate([x_prompt.reshape(mp, d), x_sample.reshape(ms, d)], axis=0)
    p_all = jnp.concatenate([p_prompt.reshape(depth, mp, -1), p_sample.reshape(depth, ms, -1)], axis=1)

    gla_p, gla_s, ret_p, ret_s = [], [], [], []
    swa_kp, swa_vp, swa_ks, swa_vs = [], [], [], []
    for i in range(depth):
        j = i // N_MIXERS
        h = _rmsnorm(x, norm_mix[i], BF16)
        if i % N_MIXERS == 0:
            rank = gla_w_a2.shape[1]
            nz = gla_w_in.shape[2] - rank
            z = _matmul(h, gla_w_in, j, n=nz, out_dtype=BF16, name="gla_in")
            alow = _matmul(h, gla_w_in[j:j + 1, :, nz:], 0, out_dtype=F32, name="gla_in_lowrank")
            o_p, s_p = _gla_recurrence(z, alow, gla_w_a2, gla_b_a, gla_norm, j, None, batch=bp, length=lp,
                                       rows=min(GLA_ROWS, lp), chunk=GLA_CHUNK, sub=GLA_SUB, valid=GLA_CHUNK)
            o_s, s_s = _gla_recurrence(_pad_rows(z[mp:], bs, ls, ls_pad), _pad_rows(alow[mp:], bs, ls, ls_pad),
                                       gla_w_a2, gla_b_a, gla_norm, j, state_gla, batch=bs, length=ls_pad,
                                       rows=ls_pad, chunk=ls_pad, sub=ls_pad, valid=ls)
            gla_p.append(s_p)
            gla_s.append(s_s)
            o = jnp.concatenate([o_p[:mp], _unpad_rows(o_s, bs, ls, ls_pad)], axis=0)
            x = _matmul(o, gla_w_out, j, resid=x, out_dtype=F32, name="gla_out")
        elif i % N_MIXERS == 1:
            z = _matmul(h, ret_w_in, j, out_dtype=BF16, name="ret_in")
            o_p, s_p = _ret_recurrence(z, pos_p, ret_norm, j, None, batch=bp, length=lp,
                                       chunk=min(RET_CHUNK, lp), valid=min(RET_CHUNK, lp))
            o_s, s_s = _ret_recurrence(_pad_rows(z[mp:], bs, ls, ls_pad), pos_s, ret_norm, j, state_ret,
                                       batch=bs, length=ls_pad, chunk=ls_pad, valid=ls)
            ret_p.append(s_p)
            ret_s.append(s_s)
            o = jnp.concatenate([o_p[:mp], _unpad_rows(o_s, bs, ls, ls_pad)], axis=0)
            x = _matmul(o, ret_w_out, j, resid=x, out_dtype=F32, name="ret_out")
        else:
            hd = d // SWA_HEADS
            n_rot = hd // 4
            nq, nkv = SWA_HEADS * hd, SWA_KV_HEADS * hd
            z = _matmul(h, swa_w_in, j, bias=swa_b_in, out_dtype=F32, name="swa_in")
            o_p, k_p = _swa_prompt(z, swa_sinks[j], batch=bp, length=lp, hd=hd, n_rot=n_rot)
            zs = _pad_rows(z[mp:], bs, ls, ls_pad).reshape(bs, ls_pad, -1)
            ck = cache_swa_k[j].reshape(bs, -1, nkv)
            cv = cache_swa_v[j].reshape(bs, -1, nkv)
            o_s, k_s = _swa_sample(zs, ck, cv, swa_sinks[j], pos_s, batch=bs, rows=ls_pad, valid=ls, hd=hd, n_rot=n_rot)
            swa_kp.append(k_p.reshape(bp, SWA_WINDOW, SWA_KV_HEADS, hd))
            swa_vp.append(z[:mp, nq + nkv:].reshape(bp, lp, SWA_KV_HEADS, hd)[:, -SWA_WINDOW:])
            swa_ks.append(k_s[:, :ls].reshape(bs, ls, SWA_KV_HEADS, hd))
            swa_vs.append(z[mp:, nq + nkv:].reshape(bs, ls, SWA_KV_HEADS, hd))
            o = jnp.concatenate([o_p[:mp], o_s[:, :ls].reshape(ms, nq)], axis=0)
            x = _matmul(o, swa_w_out, j, bias=swa_b_out, resid=x, out_dtype=F32, name="swa_out")
        u = _matmul(_rmsnorm(x, norm_mlp[i], BF16), mlp_w_up, i, act="relu2", out_dtype=BF16, name="mlp_up")
        x = _matmul(u, mlp_w_down, i, resid=x, out_dtype=F32, name="mlp_down")
        x = _matmul(_rmsnorm(x, norm_ple[i], BF16), ple_w_gate, i, gate=(p_all[i], ple_w_proj), resid=x,
                    out_dtype=F32, name="ple")
    y = _rmsnorm(x, norm_final, F32)
    return (y[:mp].reshape(bp, lp, d), y[mp:].reshape(bs, ls, d),
            jnp.stack(gla_p), jnp.stack(gla_s), jnp.stack(ret_p), jnp.stack(ret_s),
            jnp.stack(swa_kp), jnp.stack(swa_vp), jnp.stack(swa_ks), jnp.stack(swa_vs))
```

```python
import functools

import jax
import jax.numpy as jnp
from jax import lax
from jax.experimental import pallas as pl
from jax.experimental.pallas import tpu as pltpu

F32 = jnp.float32
BF16 = jnp.bfloat16

VMEM_LIMIT_BYTES = 56 * 1024 * 1024
LANES = 128
SUBLANES = 8
MXU_COLS = 256

NORM_EPS = 1e-6
PAST_LEN = 8192
N_MIXERS = 3

GLA_HEADS = 8
GLA_TAU = 16.0
GLA_CHUNK = 64
GLA_SUB = 8
GLA_ROWS = 256
GLA_HEADS_PER_STEP = 2

RET_HEADS = 16
RET_THETA = 10000.0
RET_CHUNK = 256

SWA_HEADS = 64
SWA_KV_HEADS = 8
SWA_WINDOW = 128
ROPE_THETA = 500000.0

ROW_BLOCK = 1408
COL_BLOCK = 512
DOT_COLS = 512
DEEP_ROW_BLOCK = 2112
DEEP_COL_BLOCK = 512
DEEP_K_BLOCK = 2048
NORM_ROWS = 512


def _params(*sem):
    return pltpu.CompilerParams(dimension_semantics=sem, vmem_limit_bytes=VMEM_LIMIT_BYTES)


def _row_block(m, target):
    if m <= target:
        return m
    for d in range(target, SUBLANES - 1, -1):
        if m % d == 0 and d % SUBLANES == 0:
            return d
    raise ValueError(f"no row block for {m}")


def _dot(a, b):
    return jnp.dot(a, b, preferred_element_type=F32)


def _dot_nt(a, b):
    return lax.dot_general(a, b, (((1,), (1,)), ((), ())), preferred_element_type=F32)


def _dot_tn(a, b):
    return lax.dot_general(a, b, (((0,), (0,)), ((), ())), preferred_element_type=F32)


def _silu(x):
    return x * jax.nn.sigmoid(x)


def _rmsnorm_kernel(x_ref, g_ref, o_ref):
    x = x_ref[...]
    ms = jnp.mean(x * x, axis=-1, keepdims=True)
    o_ref[...] = (x * lax.rsqrt(ms + NORM_EPS) * g_ref[...]).astype(o_ref.dtype)


def _rmsnorm(x, g, out_dtype, row0=0, nrows=None):
    d = x.shape[1]
    nrows = x.shape[0] if nrows is None else nrows
    bm = _row_block(nrows, NORM_ROWS)
    assert row0 % bm == 0
    off = row0 // bm
    return pl.pallas_call(
        _rmsnorm_kernel,
        grid=(nrows // bm,),
        in_specs=[pl.BlockSpec((bm, d), lambda i: (off + i, 0)), pl.BlockSpec((1, d), lambda i: (0, 0))],
        out_specs=pl.BlockSpec((bm, d), lambda i: (i, 0)),
        out_shape=jax.ShapeDtypeStruct((nrows, d), out_dtype),
        compiler_params=_params("parallel"),
        name="rmsnorm",
    )(x, g.reshape(1, d))


def _matmul_kernel(*refs, nk, ncol, has_bias, has_resid, has_gate, act, w_t):
    it = iter(refs)
    x_ref = next(it)
    w_ref = next(it)
    b_ref = next(it) if has_bias else None
    r_ref = next(it) if has_resid else None
    p_ref = next(it) if has_gate else None
    wp_ref = next(it) if has_gate else None
    o_ref = next(it)

    bn = o_ref.shape[1]
    cw = bn // ncol
    if nk > 1:
        @pl.when(pl.program_id(2) == 0)
        def _():
            acc = r_ref[...]
            if has_bias:
                acc = acc + b_ref[...]
            o_ref[...] = acc

    x = x_ref[...]
    for c in range(ncol):
        cs = slice(c * cw, (c + 1) * cw)
        if w_t:
            part = _dot_nt(x, w_ref[cs, :].astype(BF16))
        else:
            part = _dot(x, w_ref[:, cs].astype(BF16))

        if nk == 1:
            acc = part
            if has_bias:
                acc = acc + b_ref[:, cs]
            if act == "relu2":
                r = jnp.maximum(acc, 0.0)
                acc = r * r
            if has_gate:
                proj = _dot(p_ref[...].astype(BF16), wp_ref[:, cs].astype(BF16))
                acc = jax.nn.sigmoid(acc) * proj
            if has_resid:
                acc = r_ref[:, cs] + acc
            o_ref[:, cs] = acc.astype(o_ref.dtype)
        else:
            o_ref[:, cs] += part


def _matmul(x, w, layer, *, n=None, col0=0, w_t=False, bias=None, resid=None, gate=None, act=None, out_dtype, name):
    m, kdim = x.shape
    n_total = w.shape[1] if w_t else w.shape[2]
    n = n_total if n is None else n
    if kdim <= DEEP_K_BLOCK * 2:
        bm, bk, bn = _row_block(m, ROW_BLOCK), kdim, min(n, COL_BLOCK)
        ncol = max(1, bn // DOT_COLS)
    else:
        bm, bk, bn = _row_block(m, DEEP_ROW_BLOCK), DEEP_K_BLOCK, min(n, DEEP_COL_BLOCK)
        ncol = 1
        assert act is None and gate is None and out_dtype == F32 and resid is not None
    assert n % bn == 0 and kdim % bk == 0 and m % bm == 0 and col0 % bn == 0
    nk = kdim // bk
    c0 = col0 // bn
    grid = (m // bm, n // bn, nk)

    if nk == 1:
        x_spec = pl.BlockSpec((bm, bk), lambda i, j, k: (i, k), pipeline_mode=pl.Buffered(1))
    else:
        x_spec = pl.BlockSpec((bm, bk), lambda i, j, k: (i, k))
    if w_t:
        w_spec = pl.BlockSpec((None, bn, bk), lambda i, j, k: (layer, c0 + j, k))
    else:
        w_spec = pl.BlockSpec((None, bk, bn), lambda i, j, k: (layer, k, c0 + j))
    in_specs = [x_spec, w_spec]
    args = [x, w]
    if bias is not None:
        in_specs.append(pl.BlockSpec((None, 1, bn), lambda i, j, k: (layer, 0, j)))
        args.append(bias.reshape(bias.shape[0], 1, bias.shape[1]))
    if resid is not None:
        in_specs.append(pl.BlockSpec((bm, bn), lambda i, j, k: (i, j)))
        args.append(resid)
    if gate is not None:
        p, wp = gate
        kp = p.shape[1]
        in_specs.append(pl.BlockSpec((bm, kp), lambda i, j, k: (i, 0)))
        in_specs.append(pl.BlockSpec((None, kp, bn), lambda i, j, k: (layer, 0, j)))
        args += [p, wp]
    kern = functools.partial(_matmul_kernel, nk=nk, ncol=ncol, has_bias=bias is not None, has_resid=resid is not None,
                             has_gate=gate is not None, act=act, w_t=w_t)
    return pl.pallas_call(
        kern,
        grid=grid,
        in_specs=in_specs,
        out_specs=pl.BlockSpec((bm, bn), lambda i, j, k: (i, j)),
        out_shape=jax.ShapeDtypeStruct((m, n), out_dtype),
        compiler_params=_params("parallel", "parallel", "arbitrary"),
        name=name,
    )(*args)


def _mixer_call(kern, name, grid_order, *, batch, heads, hps, nt, rows, dk, dv, row0, layer, n_layers,
                in_specs, args, s0, o_prev, s_prev, m_total, extra_scratch=()):
    rb0 = row0 // rows
    assert row0 % rows == 0 and heads % hps == 0

    def ix(f):
        if grid_order == "bht":
            return lambda b, hg, t: f(b, hg, t)
        return lambda hg, b, t: f(b, hg, t)

    in_specs = [pl.BlockSpec(bs, ix(f)) for bs, f in in_specs]
    args = list(args)
    aliases = {}
    if s0 is not None:
        in_specs.append(pl.BlockSpec((None, None, hps, dk, dv), ix(lambda b, hg, t: (layer, b, hg, 0, 0))))
        args.append(s0)
    if o_prev is not None:
        aliases[len(args)] = 0
        in_specs.append(pl.BlockSpec(memory_space=pl.ANY))
        args.append(o_prev)
    if s_prev is not None:
        aliases[len(args)] = 1
        in_specs.append(pl.BlockSpec(memory_space=pl.ANY))
        args.append(s_prev)
    grid = (batch, heads // hps, nt) if grid_order == "bht" else (heads // hps, batch, nt)
    return pl.pallas_call(
        kern,
        grid=grid,
        in_specs=in_specs,
        out_specs=[
            pl.BlockSpec((rows, hps * dv), ix(lambda b, hg, t: (rb0 + b * nt + t, hg))),
            pl.BlockSpec((None, None, hps, dk, dv), ix(lambda b, hg, t: (layer, b, hg, 0, 0))),
        ],
        out_shape=[
            jax.ShapeDtypeStruct((m_total, heads * dv), BF16),
            jax.ShapeDtypeStruct((n_layers, batch, heads, dk, dv), F32),
        ],
        scratch_shapes=[pltpu.VMEM((hps, dk, dv), F32), *extra_scratch],
        input_output_aliases=aliases,
        compiler_params=_params("parallel", "parallel", "arbitrary"),
        name=name,
    )(*args)


def _split3(x):
    hi = x.astype(BF16)
    r1 = x - hi.astype(F32)
    mid = r1.astype(BF16)
    lo = (r1 - mid.astype(F32)).astype(BF16)
    return hi, mid, lo


def _gla_kernel(*refs, rows, chunk, sub, valid, hps, has_s0, n_alias):
    it = iter(refs)
    q_ref, k_ref, v_ref, g_ref, al_ref, wa_ref, ba_ref, gn_ref = (next(it) for _ in range(8))
    s0_ref = next(it) if has_s0 else None
    for _ in range(n_alias):
        next(it)
    o_ref, sfin_ref, s_ref = next(it), next(it), next(it)

    dk = q_ref.shape[1] // hps
    dv = v_ref.shape[1] // hps
    t = pl.program_id(2)

    @pl.when(t == 0)
    def _():
        if has_s0:
            s_ref[...] = s0_ref[...]
        else:
            s_ref[...] = jnp.zeros_like(s_ref)

    row_c = lax.broadcasted_iota(jnp.int32, (chunk, chunk), 0)
    col_c = lax.broadcasted_iota(jnp.int32, (chunk, chunk), 1)
    tri = (row_c >= col_c).astype(BF16)
    ones = jnp.ones((chunk, LANES), BF16)
    lane = lax.broadcasted_iota(jnp.int32, (sub, chunk), 1)
    subrow = lax.broadcasted_iota(jnp.int32, (sub, dk), 0)
    chunkrow = lax.broadcasted_iota(jnp.int32, (chunk, dk), 0)

    def head_chunk(rs, hh):
        ks = slice(hh * dk, (hh + 1) * dk)
        vs = slice(hh * dv, (hh + 1) * dv)
        q = q_ref[rs, ks].astype(F32) * (dk ** -0.5)
        k = k_ref[rs, ks].astype(F32)
        v = v_ref[rs, vs]
        x = _dot(al_ref[rs, :], wa_ref[:, ks]) + ba_ref[:, ks]
        la = (jnp.minimum(x, 0.0) - jnp.log1p(jnp.exp(-jnp.abs(x)))) * (1.0 / GLA_TAU)
        if valid < chunk:
            la = jnp.where(chunkrow < valid, la, 0.0)
            k = jnp.where(chunkrow < valid, k, 0.0)
        parts = _split3(la)
        b = sum(_dot(tri, p) for p in parts)
        b_tot = sum(_dot_tn(p, ones) for p in parts)

        s_old = s_ref[hh]
        o = _dot((q * jnp.exp(b)).astype(BF16), s_old.astype(BF16))

        blocks = []
        for i in range(chunk // sub):
            lo_r = i * sub
            b_r = b[lo_r:lo_r + sub, :]
            q_r = q[lo_r:lo_r + sub, :]
            acc = jnp.zeros((sub, chunk), F32)
            if i > 0:
                ref = b[lo_r - 1:lo_r, :]
                qt = (q_r * jnp.exp(b_r - ref)).astype(BF16)
                kt = k[:lo_r, :] * jnp.exp(ref - b[:lo_r, :])
                kt = jnp.concatenate([kt, jnp.zeros((chunk - lo_r, dk), F32)], axis=0)
                acc = _dot_nt(qt, kt.astype(BF16))
            for s in range(sub):
                row = lo_r + s
                e = jnp.exp(jnp.where(subrow >= s, b_r - b[row:row + 1, :], -jnp.inf))
                col = jnp.sum(q_r * k[row:row + 1, :] * e, axis=-1, keepdims=True)
                acc = acc + jnp.where(lane == row, col, 0.0)
            blocks.append(acc)
        scores = blocks[0] if len(blocks) == 1 else jnp.concatenate(blocks, axis=0)
        o = o + _dot(scores.astype(BF16), v)

        kd = (k * jnp.exp(b[chunk - 1:chunk, :] - b)).astype(BF16)
        dec = jnp.exp(b_tot)
        dec = jnp.concatenate([dec] * (dv // LANES), axis=1)
        s_ref[hh] = s_old * dec + _dot_tn(kd, v)

        ms = jnp.mean(o * o, axis=-1, keepdims=True)
        on = o * lax.rsqrt(ms + NORM_EPS) * gn_ref[...]
        o_ref[rs, vs] = (_silu(g_ref[rs, vs].astype(F32)) * on).astype(o_ref.dtype)

    def chunk_body(ci, carry):
        rs = pl.ds(pl.multiple_of(ci * chunk, chunk), chunk)
        for hh in range(hps):
            head_chunk(rs, hh)
        return carry

    if rows == chunk:
        for hh in range(hps):
            head_chunk(slice(0, chunk), hh)
    else:
        lax.fori_loop(0, rows // chunk, chunk_body, 0)

    @pl.when(t == pl.num_programs(2) - 1)
    def _():
        sfin_ref[...] = s_ref[...]


def _gla_recurrence(z, alow, w_a2, b_a, g_norm, layer, s0, o_prev, s_prev, *, batch, length, row0, rows, chunk, sub, valid):
    h, hps = GLA_HEADS, GLA_HEADS_PER_STEP
    n_layers = w_a2.shape[0]
    dk = w_a2.shape[2] // h
    dv = g_norm.shape[1]
    nt = length // rows
    assert length % rows == 0 and rows % chunk == 0 and chunk % sub == 0 and row0 % rows == 0
    rb0 = row0 // rows
    hg_n = h // hps
    kcol, vcol, gcol = hg_n, (2 * h * dk) // (hps * dv), (2 * h * dk) // (hps * dv) + hg_n

    def rowblk(b, t):
        return rb0 + b * nt + t

    in_specs = [
        ((rows, hps * dk), lambda b, hg, t: (rowblk(b, t), hg)),
        ((rows, hps * dk), lambda b, hg, t: (rowblk(b, t), kcol + hg)),
        ((rows, hps * dv), lambda b, hg, t: (rowblk(b, t), vcol + hg)),
        ((rows, hps * dv), lambda b, hg, t: (rowblk(b, t), gcol + hg)),
        ((rows, alow.shape[1]), lambda b, hg, t: (rowblk(b, t), 0)),
        ((None, w_a2.shape[1], hps * dk), lambda b, hg, t: (layer, 0, hg)),
        ((None, 1, hps * dk), lambda b, hg, t: (layer, 0, hg)),
        ((None, 1, dv), lambda b, hg, t: (layer, 0, 0)),
    ]
    args = [z, z, z, z, alow, w_a2, b_a.reshape(n_layers, 1, -1), g_norm.reshape(n_layers, 1, -1)]
    n_alias = (o_prev is not None) + (s_prev is not None)
    kern = functools.partial(_gla_kernel, rows=rows, chunk=chunk, sub=sub, valid=valid, hps=hps,
                             has_s0=s0 is not None, n_alias=n_alias)
    return _mixer_call(kern, "gla_recurrence", "bht", batch=batch, heads=h, hps=hps, nt=nt, rows=rows, dk=dk, dv=dv,
                       row0=row0, layer=layer, n_layers=n_layers, in_specs=in_specs, args=args, s0=s0,
                       o_prev=o_prev, s_prev=s_prev, m_total=z.shape[0])


def _ret_kernel(*refs, valid, has_s0, n_alias):
    it = iter(refs)
    q_ref, k_ref, v_ref, g_ref, cos_ref, sin_ref, lg_ref, gn_ref = (next(it) for _ in range(8))
    s0_ref = next(it) if has_s0 else None
    for _ in range(n_alias):
        next(it)
    o_ref, sfin_ref, s_ref = next(it), next(it), next(it)

    c, dk = q_ref.shape
    dv = v_ref.shape[1]
    half = dk // 2
    t = pl.program_id(2)

    @pl.when(t == 0)
    def _():
        if has_s0:
            s_ref[...] = s0_ref[...]
        else:
            s_ref[...] = jnp.zeros_like(s_ref)

    cos = cos_ref[...]
    sin = sin_ref[...]

    def rot(x):
        x1, x2 = x[:, :half], x[:, half:]
        return jnp.concatenate([x1 * cos - x2 * sin, x2 * cos + x1 * sin], axis=1)

    qr = rot(q_ref[...].astype(F32))
    kr = rot(k_ref[...].astype(F32)) * (dk ** -0.5)
    if valid < c:
        kr = jnp.where(lax.broadcasted_iota(jnp.int32, (c, dk), 0) < valid, kr, 0.0)
    v = v_ref[...]

    lg = lg_ref[...]
    ti = lax.broadcasted_iota(jnp.int32, (c, half), 0).astype(F32)
    q_dec = jnp.exp(lg * (ti + 1.0))
    k_dec = jnp.exp(lg * (valid - 1.0 - ti))
    q_dec = jnp.concatenate([q_dec] * (dk // half), axis=1)
    k_dec = jnp.concatenate([k_dec] * (dk // half), axis=1)
    rel = (lax.broadcasted_iota(jnp.int32, (c, c), 0) - lax.broadcasted_iota(jnp.int32, (c, c), 1)).astype(F32)
    dmask = jnp.where(rel >= 0.0, jnp.exp(lg[:, :1] * jnp.maximum(rel, 0.0)), 0.0)

    scores = _dot_nt(qr.astype(BF16), kr.astype(BF16)) * dmask
    s_old = s_ref[0]
    o = _dot(scores.astype(BF16), v) + _dot((qr * q_dec).astype(BF16), s_old.astype(BF16))
    s_dec = jnp.concatenate([jnp.exp(lg * float(valid))] * (dv // LANES), axis=1)
    s_ref[0] = s_old * s_dec + _dot_tn((kr * k_dec).astype(BF16), v)

    mu = jnp.mean(o, axis=-1, keepdims=True)
    d = o - mu
    var = jnp.mean(d * d, axis=-1, keepdims=True)
    on = d * lax.rsqrt(var + NORM_EPS) * gn_ref[...]
    o_ref[...] = (_silu(g_ref[...].astype(F32)) * on).astype(o_ref.dtype)

    @pl.when(t == pl.num_programs(2) - 1)
    def _():
        sfin_ref[...] = s_ref[...]


def _ret_recurrence(z, pos, g_norm, layer, s0, o_prev, s_prev, *, batch, length, row0, chunk, valid):
    h = RET_HEADS
    n_layers = g_norm.shape[0]
    dv = g_norm.shape[1] // h
    dk = (z.shape[1] - 2 * h * dv) // (2 * h)
    half = dk // 2
    nt = length // chunk
    assert length % chunk == 0 and half == LANES and row0 % chunk == 0
    rb0 = row0 // chunk
    kcol, vcol, gcol = h, (2 * h * dk) // dv, (2 * h * dk) // dv + h

    inv_freq = RET_THETA ** (-jnp.arange(half, dtype=F32) / half)
    ang = pos.astype(F32)[:, None] * inv_freq[None, :]
    cos, sin = jnp.cos(ang), jnp.sin(ang)
    log_gamma = jnp.log1p(-jnp.exp2(-5.0 - jnp.arange(h, dtype=F32)))
    lg = jnp.broadcast_to(log_gamma[:, None, None], (h, 1, LANES))

    def rowblk(b, t):
        return rb0 + b * nt + t

    in_specs = [
        ((chunk, dk), lambda b, hh, t: (rowblk(b, t), hh)),
        ((chunk, dk), lambda b, hh, t: (rowblk(b, t), kcol + hh)),
        ((chunk, dv), lambda b, hh, t: (rowblk(b, t), vcol + hh)),
        ((chunk, dv), lambda b, hh, t: (rowblk(b, t), gcol + hh)),
        ((chunk, half), lambda b, hh, t: (t, 0)),
        ((chunk, half), lambda b, hh, t: (t, 0)),
        ((None, 1, LANES), lambda b, hh, t: (hh, 0, 0)),
        ((None, 1, dv), lambda b, hh, t: (layer, 0, hh)),
    ]
    args = [z, z, z, z, cos, sin, lg, g_norm.reshape(n_layers, 1, -1)]
    n_alias = (o_prev is not None) + (s_prev is not None)
    kern = functools.partial(_ret_kernel, valid=valid, has_s0=s0 is not None, n_alias=n_alias)
    return _mixer_call(kern, "retention_recurrence", "hbt", batch=batch, heads=h, hps=1, nt=nt, rows=chunk, dk=dk, dv=dv,
                       row0=row0, layer=layer, n_layers=n_layers, in_specs=in_specs, args=args, s0=s0,
                       o_prev=o_prev, s_prev=s_prev, m_total=z.shape[0])


def _rope_tables(pos, hd, n_rot):
    half = n_rot // 2
    inv_freq = ROPE_THETA ** (-jnp.arange(half, dtype=F32) / half)
    ang = pos.astype(F32)[:, None] * inv_freq[None, :]
    cos, sin = jnp.cos(ang), jnp.sin(ang)
    n = pos.shape[0]
    ones = jnp.ones((n, hd - n_rot), F32)
    zeros = jnp.zeros((n, hd - n_rot), F32)
    zh = jnp.zeros((n, half), F32)
    a = jnp.concatenate([cos, cos, ones], axis=1)
    b = jnp.concatenate([-sin, zh, zeros], axis=1)
    c = jnp.concatenate([zh, sin, zeros], axis=1)
    rep = LANES // hd
    return jnp.tile(a, (1, rep)), jnp.tile(b, (1, rep)), jnp.tile(c, (1, rep))


def _div_pow2(x, d):
    assert d & (d - 1) == 0
    return x >> (d.bit_length() - 1)


def _mod_pow2(x, d):
    assert d & (d - 1) == 0
    return x & (d - 1)


def _rope128(x, a, b, c, half):
    return x * a + pltpu.roll(x, LANES - half, 1) * b + pltpu.roll(x, half, 1) * c


def _swa_prompt_kernel(sinks_ref, q_ref, k_ref, v_ref, a_ref, b_ref, c_ref, o_ref, krot_ref,
                       qg_ref, kcat_ref, vext_ref, *, hd, half):
    w = q_ref.shape[0]
    nq = q_ref.shape[1]
    nkv = k_ref.shape[1]
    n_heads, n_kv = nq // hd, nkv // hd
    group = n_heads // n_kv
    per = LANES // hd
    i = pl.program_id(1)

    @pl.when(i == 0)
    def _():
        kcat_ref[0:w, :] = jnp.zeros((w, nkv), BF16)
        vext_ref[:, 0:w, 0:hd] = jnp.zeros((n_kv, w, hd), BF16)
        vext_ref[:, :, hd:2 * hd] = jnp.ones((n_kv, 2 * w, hd), BF16)

    a, b, c = a_ref[...], b_ref[...], c_ref[...]
    scale = hd ** -0.5
    for m in range(nq // LANES):
        r = (_rope128(q_ref[:, m * LANES:(m + 1) * LANES], a, b, c, half) * scale).astype(BF16)
        for u in range(per):
            hq = m * per + u
            kv, gq = hq // group, hq % group
            qg_ref[kv, gq * w:(gq + 1) * w, :] = r[:, u * hd:(u + 1) * hd]
    for m in range(nkv // LANES):
        sl = slice(m * LANES, (m + 1) * LANES)
        kr = _rope128(k_ref[:, sl], a, b, c, half)
        krot_ref[:, sl] = kr
        kcat_ref[w:2 * w, sl] = kr.astype(BF16)
    vb = v_ref[...].astype(BF16)
    for kv in range(n_kv):
        vext_ref[kv, w:2 * w, 0:hd] = vb[:, kv * hd:(kv + 1) * hd]

    qi = lax.broadcasted_iota(jnp.int32, (w, 2 * w), 0)
    kj = lax.broadcasted_iota(jnp.int32, (w, 2 * w), 1)
    rel = qi + w - kj
    first_key = jnp.where(i > 0, 0, w)
    mask = (rel >= 0) & (rel <= w) & (kj >= first_key)
    bias = jnp.where(mask, 0.0, -jnp.inf)

    s_all = [_dot_nt(qg_ref[kv], kcat_ref[:, kv * hd:(kv + 1) * hd]) for kv in range(n_kv)]
    ps, ms = [], []
    for kv in range(n_kv):
        for gq in range(group):
            sink = sinks_ref[kv * group + gq]
            s = s_all[kv][gq * w:(gq + 1) * w, :] + bias
            m = jnp.maximum(jnp.max(s, axis=-1, keepdims=True), sink)
            ps.append(jnp.exp(s - m).astype(BF16))
            ms.append(jnp.exp(sink - m))
    oe = [_dot(jnp.concatenate(ps[kv * group:(kv + 1) * group], axis=0), vext_ref[kv]) for kv in range(n_kv)]
    for kv in range(n_kv):
        for gq in range(group):
            hq = kv * group + gq
            blk = oe[kv][gq * w:(gq + 1) * w, :]
            denom = blk[:, hd:2 * hd] + ms[hq]
            o_ref[:, hq * hd:(hq + 1) * hd] = (blk[:, 0:hd] / denom).astype(o_ref.dtype)

    kcat_ref[0:w, :] = kcat_ref[w:2 * w, :]
    vext_ref[:, 0:w, 0:hd] = vext_ref[:, w:2 * w, 0:hd]


def _swa_prompt(z, sinks, *, batch, length, hd, n_rot):
    w = SWA_WINDOW
    nq, nkv = SWA_HEADS * hd, SWA_KV_HEADS * hd
    nb = length // w
    group = SWA_HEADS // SWA_KV_HEADS
    assert length % w == 0 and nq % nkv == 0 and LANES % hd == 0
    a, b, c = _rope_tables(jnp.arange(length, dtype=jnp.int32), hd, n_rot)
    kcol = nq // nkv
    tab = pl.BlockSpec((w, LANES), lambda bb, i: (i, 0))
    kern = functools.partial(_swa_prompt_kernel, hd=hd, half=n_rot // 2)
    return pl.pallas_call(
        kern,
        grid=(batch, nb),
        in_specs=[
            pl.BlockSpec(memory_space=pltpu.SMEM),
            pl.BlockSpec((w, nq), lambda bb, i: (bb * nb + i, 0)),
            pl.BlockSpec((w, nkv), lambda bb, i: (bb * nb + i, kcol)),
            pl.BlockSpec((w, nkv), lambda bb, i: (bb * nb + i, kcol + 1)),
            tab, tab, tab,
        ],
        out_specs=[
            pl.BlockSpec((w, nq), lambda bb, i: (bb * nb + i, 0)),
            pl.BlockSpec((None, w, nkv), lambda bb, i: (bb, 0, 0)),
        ],
        out_shape=[
            jax.ShapeDtypeStruct((z.shape[0], nq), BF16),
            jax.ShapeDtypeStruct((batch, w, nkv), F32),
        ],
        scratch_shapes=[pltpu.VMEM((SWA_KV_HEADS, group * w, hd), BF16), pltpu.VMEM((2 * w, nkv), BF16),
                        pltpu.VMEM((SWA_KV_HEADS, 2 * w, 2 * hd), BF16)],
        compiler_params=_params("parallel", "arbitrary"),
        name="swa_prompt",
    )(sinks, z, z, z, a, b, c)


def _swa_sample_kernel(q_ref, k_ref, v_ref, ck_ref, cv_ref, a_ref, b_ref, c_ref, sink_ref, o_prev_ref, o_ref, krot_ref,
                       qall_ref, kall_ref, vall_ref, knew_ref, vnew_ref, *, hd, half, valid):
    del o_prev_ref
    rows = q_ref.shape[0]
    w = ck_ref.shape[0]
    nq = q_ref.shape[1]
    nkv = k_ref.shape[1]
    n_heads, n_kv = nq // hd, nkv // hd
    group = n_heads // n_kv
    per = LANES // hd

    a, b, c = a_ref[...], b_ref[...], c_ref[...]
    scale = hd ** -0.5
    for m in range(nq // LANES):
        r = _rope128(q_ref[:, m * LANES:(m + 1) * LANES], a, b, c, half) * scale
        for u in range(per):
            hq = m * per + u
            qall_ref[hq * rows:(hq + 1) * rows, :] = r[:, u * hd:(u + 1) * hd]
    for m in range(nkv // LANES):
        sl = slice(m * LANES, (m + 1) * LANES)
        kr = _rope128(k_ref[:, sl], a, b, c, half)
        krot_ref[:, sl] = kr
        for u in range(per):
            kv = m * per + u
            knew_ref[kv * rows:(kv + 1) * rows, :] = kr[:, u * hd:(u + 1) * hd]
    for kv in range(n_kv):
        ksl = slice(kv * hd, (kv + 1) * hd)
        kall_ref[kv * w:(kv + 1) * w, :] = ck_ref[:, ksl].astype(BF16)
        vall_ref[kv * w:(kv + 1) * w, :] = cv_ref[:, ksl].astype(BF16)
        vnew_ref[kv * rows:(kv + 1) * rows, :] = v_ref[:, ksl]

    nrow = n_heads * rows
    qa = qall_ref[...].astype(BF16)
    sc = _dot_nt(qa, kall_ref[...])
    sn = _dot_nt(qa, knew_ref[...].astype(BF16))
    rc = lax.broadcasted_iota(jnp.int32, (nrow, n_kv * w), 0)
    cc = lax.broadcasted_iota(jnp.int32, (nrow, n_kv * w), 1)
    mask_c = (_div_pow2(cc, w) == _div_pow2(rc, group * rows)) & (_mod_pow2(cc, w) >= _mod_pow2(rc, rows))
    rn = lax.broadcasted_iota(jnp.int32, (nrow, n_kv * rows), 0)
    cn = lax.broadcasted_iota(jnp.int32, (nrow, n_kv * rows), 1)
    jn = _mod_pow2(cn, rows)
    mask_n = (_div_pow2(cn, rows) == _div_pow2(rn, group * rows)) & (jn <= _mod_pow2(rn, rows)) & (jn < valid)
    sc = jnp.where(mask_c, sc, -jnp.inf)
    sn = jnp.where(mask_n, sn, -jnp.inf)
    sink = sink_ref[...]
    m = jnp.maximum(jnp.maximum(jnp.max(sc, axis=-1, keepdims=True), jnp.max(sn, axis=-1, keepdims=True)), sink)
    pc = jnp.exp(sc - m)
    pn = jnp.exp(sn - m)
    denom = jnp.sum(pc, axis=-1, keepdims=True) + jnp.sum(pn, axis=-1, keepdims=True) + jnp.exp(sink - m)
    o = (_dot(pc.astype(BF16), vall_ref[...]) + _dot(pn.astype(BF16), vnew_ref[...].astype(BF16))) / denom
    for hq in range(n_heads):
        o_ref[:, hq * hd:(hq + 1) * hd] = o[hq * rows:(hq + 1) * rows, :].astype(o_ref.dtype)


def _swa_sample(z, ck, cv, sinks, pos, o_prev, *, batch, row0, rows, valid, hd, n_rot):
    nq, nkv = SWA_HEADS * hd, SWA_KV_HEADS * hd
    w = ck.shape[1]
    a, b, c = _rope_tables(pos, hd, n_rot)
    kcol = nq // nkv
    rb0 = row0 // rows
    nrow = SWA_HEADS * rows
    sink_col = jnp.repeat(sinks, rows)[:, None]
    tab = pl.BlockSpec((rows, LANES), lambda bb: (0, 0))
    kern = functools.partial(_swa_sample_kernel, hd=hd, half=n_rot // 2, valid=valid)
    return pl.pallas_call(
        kern,
        grid=(batch,),
        in_specs=[
            pl.BlockSpec((rows, nq), lambda bb: (rb0 + bb, 0)),
            pl.BlockSpec((rows, nkv), lambda bb: (rb0 + bb, kcol)),
            pl.BlockSpec((rows, nkv), lambda bb: (rb0 + bb, kcol + 1)),
            pl.BlockSpec((None, w, nkv), lambda bb: (bb, 0, 0)),
            pl.BlockSpec((None, w, nkv), lambda bb: (bb, 0, 0)),
            tab, tab, tab,
            pl.BlockSpec((nrow, 1), lambda bb: (0, 0)),
            pl.BlockSpec(memory_space=pl.ANY),
        ],
        out_specs=[
            pl.BlockSpec((rows, nq), lambda bb: (rb0 + bb, 0)),
            pl.BlockSpec((None, rows, nkv), lambda bb: (bb, 0, 0)),
        ],
        out_shape=[
            jax.ShapeDtypeStruct((z.shape[0], nq), BF16),
            jax.ShapeDtypeStruct((batch, rows, nkv), F32),
        ],
        scratch_shapes=[pltpu.VMEM((nrow, hd), F32), pltpu.VMEM((SWA_KV_HEADS * w, hd), BF16),
                        pltpu.VMEM((SWA_KV_HEADS * w, hd), BF16), pltpu.VMEM((SWA_KV_HEADS * rows, hd), F32),
                        pltpu.VMEM((SWA_KV_HEADS * rows, hd), F32)],
        input_output_aliases={9: 0},
        compiler_params=_params("parallel"),
        name="swa_sample",
    )(z, z, z, ck, cv, a, b, c, sink_col, o_prev)


def _merge_rows(xp, xs, rows):
    n = xp.shape[-1]
    xs = jnp.pad(xs, ((0, 0), (0, rows - xs.shape[1]), (0, 0)))
    return jnp.concatenate([xp.reshape(-1, n), xs.reshape(-1, n)], axis=0)


def kernel(x_prompt, x_sample, state_gla, state_ret, cache_swa_k, cache_swa_v, p_prompt, p_sample, norm_mix, norm_mlp, norm_ple, norm_final, gla_w_in, gla_w_a2, gla_b_a, gla_norm, gla_w_out, ret_w_in, ret_norm, ret_w_out, swa_w_in, swa_b_in, swa_sinks, swa_w_out, swa_b_out, mlp_w_up, mlp_w_down, ple_w_gate, ple_w_proj):
    bp, lp, d = x_prompt.shape
    bs, ls, _ = x_sample.shape
    depth = norm_mix.shape[0]
    lpad = SUBLANES
    mp, ms = bp * lp, bs * lpad
    pos_p = jnp.arange(lp, dtype=jnp.int32)
    pos_s = PAST_LEN + jnp.arange(lpad, dtype=jnp.int32)

    x = _merge_rows(x_prompt, x_sample, lpad)
    p_all = jnp.stack([_merge_rows(p_prompt[i], p_sample[i], lpad) for i in range(depth)])
    gla_w_in_t = jnp.swapaxes(gla_w_in, 1, 2)

    gla_p = gla_s = ret_p = ret_s = None
    swa_kp, swa_vp, swa_ks, swa_vs = [], [], [], []
    for i in range(depth):
        j = i // N_MIXERS
        h = _rmsnorm(x, norm_mix[i], BF16)
        if i % N_MIXERS == 0:
            rank = gla_w_a2.shape[1]
            nz = gla_w_in.shape[2] - rank
            z = _matmul(h, gla_w_in_t, j, n=nz, w_t=True, out_dtype=BF16, name="gla_in")
            alow = _matmul(h, gla_w_in_t, j, n=rank, col0=nz, w_t=True, out_dtype=F32, name="gla_in_lowrank")
            o, gla_p = _gla_recurrence(z, alow, gla_w_a2, gla_b_a, gla_norm, j, None, None, gla_p, batch=bp, length=lp,
                                       row0=0, rows=min(GLA_ROWS, lp), chunk=GLA_CHUNK, sub=GLA_SUB, valid=GLA_CHUNK)
            o, gla_s = _gla_recurrence(z, alow, gla_w_a2, gla_b_a, gla_norm, j, state_gla, o, gla_s, batch=bs,
                                       length=lpad, row0=mp, rows=lpad, chunk=lpad, sub=lpad, valid=ls)
            x = _matmul(o, gla_w_out, j, resid=x, out_dtype=F32, name="gla_out")
        elif i % N_MIXERS == 1:
            z = _matmul(h, ret_w_in, j, out_dtype=BF16, name="ret_in")
            cp = min(RET_CHUNK, lp)
            o, ret_p = _ret_recurrence(z, pos_p, ret_norm, j, None, None, ret_p, batch=bp, length=lp, row0=0,
                                       chunk=cp, valid=cp)
            o, ret_s = _ret_recurrence(z, pos_s, ret_norm, j, state_ret, o, ret_s, batch=bs, length=lpad, row0=mp,
                                       chunk=lpad, valid=ls)
            x = _matmul(o, ret_w_out, j, resid=x, out_dtype=F32, name="ret_out")
        else:
            hd = d // SWA_HEADS
            n_rot = hd // 4
            nq, nkv = SWA_HEADS * hd, SWA_KV_HEADS * hd
            z = _matmul(h, swa_w_in, j, bias=swa_b_in, out_dtype=F32, name="swa_in")
            o, k_p = _swa_prompt(z, swa_sinks[j], batch=bp, length=lp, hd=hd, n_rot=n_rot)
            ck = cache_swa_k[j].reshape(bs, -1, nkv)
            cv = cache_swa_v[j].reshape(bs, -1, nkv)
            o, k_s = _swa_sample(z, ck, cv, swa_sinks[j], pos_s, o, batch=bs, row0=mp, rows=lpad, valid=ls, hd=hd, n_rot=n_rot)
            v_new = z[:, nq + nkv:]
            swa_kp.append(k_p.reshape(bp, SWA_WINDOW, SWA_KV_HEADS, hd))
            swa_vp.append(v_new[:mp].reshape(bp, lp, SWA_KV_HEADS, hd)[:, -SWA_WINDOW:])
            swa_ks.append(k_s[:, :ls].reshape(bs, ls, SWA_KV_HEADS, hd))
            swa_vs.append(v_new[mp:].reshape(bs, lpad, SWA_KV_HEADS, hd)[:, :ls])
            x = _matmul(o, swa_w_out, j, bias=swa_b_out, resid=x, out_dtype=F32, name="swa_out")
        u = _matmul(_rmsnorm(x, norm_mlp[i], BF16), mlp_w_up, i, act="relu2", out_dtype=BF16, name="mlp_up")
        x = _matmul(u, mlp_w_down, i, resid=x, out_dtype=F32, name="mlp_down")
        x = _matmul(_rmsnorm(x, norm_ple[i], BF16), ple_w_gate, i, gate=(p_all[i], ple_w_proj), resid=x,
                    out_dtype=F32, name="ple")
    y_p = _rmsnorm(x, norm_final, F32, row0=0, nrows=mp)
    y_s = _rmsnorm(x, norm_final, F32, row0=mp, nrows=ms)
    return (y_p.reshape(bp, lp, d), y_s.reshape(bs, lpad, d)[:, :ls],
            gla_p, gla_s, ret_p, ret_s,
            jnp.stack(swa_kp), jnp.stack(swa_vp), jnp.stack(swa_ks), jnp.stack(swa_vs))
```

```python
import functools

import jax
import jax.numpy as jnp
from jax import lax
from jax.experimental import pallas as pl
from jax.experimental.pallas import tpu as pltpu

F32 = jnp.float32
BF16 = jnp.bfloat16

VMEM_LIMIT_BYTES = 56 * 1024 * 1024
LANES = 128
SUBLANES = 8
MXU_COLS = 256

NORM_EPS = 1e-6
PAST_LEN = 8192
N_MIXERS = 3

GLA_HEADS = 8
GLA_TAU = 16.0
GLA_CHUNK = 64
GLA_SUB = 8
GLA_ROWS = 256
GLA_HEADS_PER_STEP = 2

RET_HEADS = 16
RET_THETA = 10000.0
RET_CHUNK = 256
RET_SAMPLE_HEADS_PER_STEP = 4

SWA_HEADS = 64
SWA_KV_HEADS = 8
SWA_WINDOW = 128
ROPE_THETA = 500000.0

ROW_BLOCK = 1408
COL_BLOCK = 512
DOT_COLS = 512
DEEP_ROW_BLOCK = 2112
DEEP_COL_BLOCK = 512
DEEP_K_BLOCK = 2048
NORM_ROWS = 512


def _params(*sem):
    return pltpu.CompilerParams(dimension_semantics=sem, vmem_limit_bytes=VMEM_LIMIT_BYTES)


def _row_block(m, target):
    if m <= target:
        return m
    for d in range(target, SUBLANES - 1, -1):
        if m % d == 0 and d % SUBLANES == 0:
            return d
    raise ValueError(f"no row block for {m}")


def _dot(a, b):
    return jnp.dot(a, b, preferred_element_type=F32)


def _dot_nt(a, b):
    return lax.dot_general(a, b, (((1,), (1,)), ((), ())), preferred_element_type=F32)


def _dot_tn(a, b):
    return lax.dot_general(a, b, (((0,), (0,)), ((), ())), preferred_element_type=F32)


def _silu(x):
    return x * jax.nn.sigmoid(x)


def _rmsnorm_kernel(x_ref, g_ref, o_ref):
    x = x_ref[...]
    ms = jnp.mean(x * x, axis=-1, keepdims=True)
    o_ref[...] = (x * lax.rsqrt(ms + NORM_EPS) * g_ref[...]).astype(o_ref.dtype)


def _rmsnorm(x, g, out_dtype, row0=0, nrows=None):
    d = x.shape[1]
    nrows = x.shape[0] if nrows is None else nrows
    bm = _row_block(nrows, NORM_ROWS)
    assert row0 % bm == 0
    off = row0 // bm
    return pl.pallas_call(
        _rmsnorm_kernel,
        grid=(nrows // bm,),
        in_specs=[pl.BlockSpec((bm, d), lambda i: (off + i, 0)), pl.BlockSpec((1, d), lambda i: (0, 0))],
        out_specs=pl.BlockSpec((bm, d), lambda i: (i, 0)),
        out_shape=jax.ShapeDtypeStruct((nrows, d), out_dtype),
        compiler_params=_params("parallel"),
        name="rmsnorm",
    )(x, g.reshape(1, d))


def _matmul_kernel(*refs, nk, ncol, has_bias, has_resid, has_gate, has_scale, has_norm_out, act, w_t):
    it = iter(refs)
    x_ref = next(it)
    w_ref = next(it)
    b_ref = next(it) if has_bias else None
    r_ref = next(it) if has_resid else None
    p_ref = next(it) if has_gate else None
    wp_ref = next(it) if has_gate else None
    ss_ref = next(it) if has_scale else None
    gn_ref = next(it) if has_norm_out else None
    o_ref = next(it)
    xg_ref = next(it) if has_norm_out else None
    ssq_ref = next(it) if has_norm_out else None

    bn = o_ref.shape[1]
    cw = bn // ncol
    if has_scale:
        row_scale = lax.rsqrt(ss_ref[...] * (1.0 / x_ref.shape[1]) + NORM_EPS)

    def emit_norm_out(val, cs):
        xg_ref[:, cs] = (val * gn_ref[:, cs]).astype(BF16)
        sq = jnp.sum(val * val, axis=-1, keepdims=True)
        first = (pl.program_id(1) == 0) & (cs.start == 0)

        @pl.when(first)
        def _():
            ssq_ref[...] = sq

        @pl.when(jnp.logical_not(first))
        def _():
            ssq_ref[...] += sq

    if nk > 1:
        @pl.when(pl.program_id(2) == 0)
        def _():
            acc = r_ref[...]
            if has_bias:
                acc = acc + b_ref[...]
            o_ref[...] = acc

    x = x_ref[...]
    for c in range(ncol):
        cs = slice(c * cw, (c + 1) * cw)
        if w_t:
            part = _dot_nt(x, w_ref[cs, :].astype(BF16))
        else:
            part = _dot(x, w_ref[:, cs].astype(BF16))

        if nk == 1:
            acc = part
            if has_scale:
                acc = acc * row_scale
            if has_bias:
                acc = acc + b_ref[:, cs]
            if act == "relu2":
                r = jnp.maximum(acc, 0.0)
                acc = r * r
            if has_gate:
                proj = _dot(p_ref[...].astype(BF16), wp_ref[:, cs].astype(BF16))
                acc = jax.nn.sigmoid(acc) * proj
            if has_resid:
                acc = r_ref[:, cs] + acc
            o_ref[:, cs] = acc.astype(o_ref.dtype)
            if has_norm_out:
                emit_norm_out(acc, cs)
        else:
            o_ref[:, cs] += part
            if has_norm_out:
                @pl.when(pl.program_id(2) == nk - 1)
                def _():
                    emit_norm_out(o_ref[:, cs], cs)


def _matmul(x, w, layer, *, n=None, col0=0, w_t=False, row_ssq=None, bias=None, resid=None, gate=None, act=None,
            norm_out=None, out_dtype, name):
    m, kdim = x.shape
    n_total = w.shape[1] if w_t else w.shape[2]
    n = n_total if n is None else n
    if kdim <= DEEP_K_BLOCK * 2:
        bm, bk, bn = _row_block(m, ROW_BLOCK), kdim, min(n, COL_BLOCK)
        ncol = max(1, bn // DOT_COLS)
    else:
        bm, bk, bn = _row_block(m, DEEP_ROW_BLOCK), DEEP_K_BLOCK, min(n, DEEP_COL_BLOCK)
        ncol = 1
        assert act is None and gate is None and out_dtype == F32 and resid is not None
    assert n % bn == 0 and kdim % bk == 0 and m % bm == 0 and col0 % bn == 0
    nk = kdim // bk
    c0 = col0 // bn
    grid = (m // bm, n // bn, nk)

    if nk == 1:
        x_spec = pl.BlockSpec((bm, bk), lambda i, j, k: (i, k), pipeline_mode=pl.Buffered(1))
    else:
        x_spec = pl.BlockSpec((bm, bk), lambda i, j, k: (i, k))
    if w_t:
        w_spec = pl.BlockSpec((None, bn, bk), lambda i, j, k: (layer, c0 + j, k))
    else:
        w_spec = pl.BlockSpec((None, bk, bn), lambda i, j, k: (layer, k, c0 + j))
    in_specs = [x_spec, w_spec]
    args = [x, w]
    if bias is not None:
        in_specs.append(pl.BlockSpec((None, 1, bn), lambda i, j, k: (layer, 0, j)))
        args.append(bias.reshape(bias.shape[0], 1, bias.shape[1]))
    if resid is not None:
        in_specs.append(pl.BlockSpec((bm, bn), lambda i, j, k: (i, j)))
        args.append(resid)
    if gate is not None:
        p, wp = gate
        kp = p.shape[1]
        in_specs.append(pl.BlockSpec((bm, kp), lambda i, j, k: (i, 0)))
        in_specs.append(pl.BlockSpec((None, kp, bn), lambda i, j, k: (layer, 0, j)))
        args += [p, wp]
    if row_ssq is not None:
        in_specs.append(pl.BlockSpec((bm, 1), lambda i, j, k: (i, 0)))
        args.append(row_ssq)
    out_specs = [pl.BlockSpec((bm, bn), lambda i, j, k: (i, j))]
    out_shape = [jax.ShapeDtypeStruct((m, n), out_dtype)]
    if norm_out is not None:
        assert out_dtype == F32
        in_specs.append(pl.BlockSpec((1, bn), lambda i, j, k: (0, j)))
        args.append(norm_out.reshape(1, n))
        out_specs += [pl.BlockSpec((bm, bn), lambda i, j, k: (i, j)), pl.BlockSpec((bm, 1), lambda i, j, k: (i, 0))]
        out_shape += [jax.ShapeDtypeStruct((m, n), BF16), jax.ShapeDtypeStruct((m, 1), F32)]
    kern = functools.partial(_matmul_kernel, nk=nk, ncol=ncol, has_bias=bias is not None, has_resid=resid is not None,
                             has_gate=gate is not None, has_scale=row_ssq is not None,
                             has_norm_out=norm_out is not None, act=act, w_t=w_t)
    col_sem = "arbitrary" if norm_out is not None else "parallel"
    outs = pl.pallas_call(
        kern,
        grid=grid,
        in_specs=in_specs,
        out_specs=out_specs,
        out_shape=out_shape,
        compiler_params=_params("parallel", col_sem, "arbitrary"),
        name=name,
    )(*args)
    return outs if norm_out is not None else outs[0]


def _mixer_call(kern, name, grid_order, *, batch, heads, hps, nt, rows, dk, dv, row0, layer, n_layers,
                in_specs, args, s0, o_prev, s_prev, m_total, extra_scratch=()):
    rb0 = row0 // rows
    assert row0 % rows == 0 and heads % hps == 0

    def ix(f):
        if grid_order == "bht":
            return lambda b, hg, t: f(b, hg, t)
        return lambda hg, b, t: f(b, hg, t)

    in_specs = [pl.BlockSpec(bs, ix(f)) for bs, f in in_specs]
    args = list(args)
    aliases = {}
    if s0 is not None:
        in_specs.append(pl.BlockSpec((None, None, hps, dk, dv), ix(lambda b, hg, t: (layer, b, hg, 0, 0))))
        args.append(s0)
    if o_prev is not None:
        aliases[len(args)] = 0
        in_specs.append(pl.BlockSpec(memory_space=pl.ANY))
        args.append(o_prev)
    if s_prev is not None:
        aliases[len(args)] = 1
        in_specs.append(pl.BlockSpec(memory_space=pl.ANY))
        args.append(s_prev)
    grid = (batch, heads // hps, nt) if grid_order == "bht" else (heads // hps, batch, nt)
    return pl.pallas_call(
        kern,
        grid=grid,
        in_specs=in_specs,
        out_specs=[
            pl.BlockSpec((rows, hps * dv), ix(lambda b, hg, t: (rb0 + b * nt + t, hg))),
            pl.BlockSpec((None, None, hps, dk, dv), ix(lambda b, hg, t: (layer, b, hg, 0, 0))),
        ],
        out_shape=[
            jax.ShapeDtypeStruct((m_total, heads * dv), BF16),
            jax.ShapeDtypeStruct((n_layers, batch, heads, dk, dv), F32),
        ],
        scratch_shapes=[pltpu.VMEM((hps, dk, dv), F32), *extra_scratch],
        input_output_aliases=aliases,
        compiler_params=_params("parallel", "parallel", "arbitrary"),
        name=name,
    )(*args)


def _split3(x):
    hi = x.astype(BF16)
    r1 = x - hi.astype(F32)
    mid = r1.astype(BF16)
    lo = (r1 - mid.astype(F32)).astype(BF16)
    return hi, mid, lo


def _gla_kernel(*refs, rows, chunk, sub, valid, hps, has_s0, n_alias):
    it = iter(refs)
    q_ref, k_ref, v_ref, g_ref, al_ref, wa_ref, ba_ref, gn_ref = (next(it) for _ in range(8))
    s0_ref = next(it) if has_s0 else None
    for _ in range(n_alias):
        next(it)
    o_ref, sfin_ref, s_ref = next(it), next(it), next(it)

    dk = q_ref.shape[1] // hps
    dv = v_ref.shape[1] // hps
    t = pl.program_id(2)

    @pl.when(t == 0)
    def _():
        if has_s0:
            s_ref[...] = s0_ref[...]
        else:
            s_ref[...] = jnp.zeros_like(s_ref)

    row_c = lax.broadcasted_iota(jnp.int32, (chunk, chunk), 0)
    col_c = lax.broadcasted_iota(jnp.int32, (chunk, chunk), 1)
    tri = (row_c >= col_c).astype(BF16)
    ones = jnp.ones((chunk, LANES), BF16)
    lane = lax.broadcasted_iota(jnp.int32, (sub, chunk), 1)
    subrow = lax.broadcasted_iota(jnp.int32, (sub, dk), 0)
    chunkrow = lax.broadcasted_iota(jnp.int32, (chunk, dk), 0)

    def head_chunk(rs, hh):
        ks = slice(hh * dk, (hh + 1) * dk)
        vs = slice(hh * dv, (hh + 1) * dv)
        q = q_ref[rs, ks].astype(F32) * (dk ** -0.5)
        k = k_ref[rs, ks].astype(F32)
        v = v_ref[rs, vs]
        x = _dot(al_ref[rs, :], wa_ref[:, ks]) + ba_ref[:, ks]
        la = (jnp.minimum(x, 0.0) - jnp.log1p(jnp.exp(-jnp.abs(x)))) * (1.0 / GLA_TAU)
        if valid < chunk:
            la = jnp.where(chunkrow < valid, la, 0.0)
            k = jnp.where(chunkrow < valid, k, 0.0)
        parts = _split3(la)
        b = sum(_dot(tri, p) for p in parts)
        b_tot = sum(_dot_tn(p, ones) for p in parts)

        s_old = s_ref[hh]
        o = _dot((q * jnp.exp(b)).astype(BF16), s_old.astype(BF16))

        blocks = []
        for i in range(chunk // sub):
            lo_r = i * sub
            b_r = b[lo_r:lo_r + sub, :]
            q_r = q[lo_r:lo_r + sub, :]
            acc = jnp.zeros((sub, chunk), F32)
            if i > 0:
                ref = b[lo_r - 1:lo_r, :]
                qt = (q_r * jnp.exp(b_r - ref)).astype(BF16)
                kt = k[:lo_r, :] * jnp.exp(ref - b[:lo_r, :])
                kt = jnp.concatenate([kt, jnp.zeros((chunk - lo_r, dk), F32)], axis=0)
                acc = _dot_nt(qt, kt.astype(BF16))
            for s in range(sub):
                row = lo_r + s
                e = jnp.exp(jnp.where(subrow >= s, b_r - b[row:row + 1, :], -jnp.inf))
                col = jnp.sum(q_r * k[row:row + 1, :] * e, axis=-1, keepdims=True)
                acc = acc + jnp.where(lane == row, col, 0.0)
            blocks.append(acc)
        scores = blocks[0] if len(blocks) == 1 else jnp.concatenate(blocks, axis=0)
        o = o + _dot(scores.astype(BF16), v)

        kd = (k * jnp.exp(b[chunk - 1:chunk, :] - b)).astype(BF16)
        dec = jnp.exp(b_tot)
        dec = jnp.concatenate([dec] * (dv // LANES), axis=1)
        s_ref[hh] = s_old * dec + _dot_tn(kd, v)

        ms = jnp.mean(o * o, axis=-1, keepdims=True)
        on = o * lax.rsqrt(ms + NORM_EPS) * gn_ref[...]
        o_ref[rs, vs] = (_silu(g_ref[rs, vs].astype(F32)) * on).astype(o_ref.dtype)

    def chunk_body(ci, carry):
        rs = pl.ds(pl.multiple_of(ci * chunk, chunk), chunk)
        for hh in range(hps):
            head_chunk(rs, hh)
        return carry

    if rows == chunk:
        for hh in range(hps):
            head_chunk(slice(0, chunk), hh)
    else:
        lax.fori_loop(0, rows // chunk, chunk_body, 0)

    @pl.when(t == pl.num_programs(2) - 1)
    def _():
        sfin_ref[...] = s_ref[...]


def _gla_recurrence(z, alow, w_a2, b_a, g_norm, layer, s0, o_prev, s_prev, *, batch, length, row0, rows, chunk, sub, valid):
    h, hps = GLA_HEADS, GLA_HEADS_PER_STEP
    n_layers = w_a2.shape[0]
    dk = w_a2.shape[2] // h
    dv = g_norm.shape[1]
    nt = length // rows
    assert length % rows == 0 and rows % chunk == 0 and chunk % sub == 0 and row0 % rows == 0
    rb0 = row0 // rows
    hg_n = h // hps
    kcol, vcol, gcol = hg_n, (2 * h * dk) // (hps * dv), (2 * h * dk) // (hps * dv) + hg_n

    def rowblk(b, t):
        return rb0 + b * nt + t

    in_specs = [
        ((rows, hps * dk), lambda b, hg, t: (rowblk(b, t), hg)),
        ((rows, hps * dk), lambda b, hg, t: (rowblk(b, t), kcol + hg)),
        ((rows, hps * dv), lambda b, hg, t: (rowblk(b, t), vcol + hg)),
        ((rows, hps * dv), lambda b, hg, t: (rowblk(b, t), gcol + hg)),
        ((rows, alow.shape[1]), lambda b, hg, t: (rowblk(b, t), 0)),
        ((None, w_a2.shape[1], hps * dk), lambda b, hg, t: (layer, 0, hg)),
        ((None, 1, hps * dk), lambda b, hg, t: (layer, 0, hg)),
        ((None, 1, dv), lambda b, hg, t: (layer, 0, 0)),
    ]
    args = [z, z, z, z, alow, w_a2, b_a.reshape(n_layers, 1, -1), g_norm.reshape(n_layers, 1, -1)]
    n_alias = (o_prev is not None) + (s_prev is not None)
    kern = functools.partial(_gla_kernel, rows=rows, chunk=chunk, sub=sub, valid=valid, hps=hps,
                             has_s0=s0 is not None, n_alias=n_alias)
    return _mixer_call(kern, "gla_recurrence", "bht", batch=batch, heads=h, hps=hps, nt=nt, rows=rows, dk=dk, dv=dv,
                       row0=row0, layer=layer, n_layers=n_layers, in_specs=in_specs, args=args, s0=s0,
                       o_prev=o_prev, s_prev=s_prev, m_total=z.shape[0])


def _ret_kernel(*refs, valid, hps, has_s0, n_alias):
    it = iter(refs)
    q_ref, k_ref, v_ref, g_ref, cos_ref, sin_ref, lg_ref, gn_ref = (next(it) for _ in range(8))
    s0_ref = next(it) if has_s0 else None
    for _ in range(n_alias):
        next(it)
    o_ref, sfin_ref, s_ref = next(it), next(it), next(it)

    c = q_ref.shape[0]
    dk = q_ref.shape[1] // hps
    dv = v_ref.shape[1] // hps
    half = dk // 2
    t = pl.program_id(2)

    @pl.when(t == 0)
    def _():
        if has_s0:
            s_ref[...] = s0_ref[...]
        else:
            s_ref[...] = jnp.zeros_like(s_ref)

    cos = cos_ref[...]
    sin = sin_ref[...]

    def rot(x):
        x1, x2 = x[:, :half], x[:, half:]
        return jnp.concatenate([x1 * cos - x2 * sin, x2 * cos + x1 * sin], axis=1)

    ti = lax.broadcasted_iota(jnp.int32, (c, half), 0).astype(F32)
    rel = (lax.broadcasted_iota(jnp.int32, (c, c), 0) - lax.broadcasted_iota(jnp.int32, (c, c), 1)).astype(F32)
    krow = lax.broadcasted_iota(jnp.int32, (c, dk), 0)

    for hh in range(hps):
        ks = slice(hh * dk, (hh + 1) * dk)
        vs = slice(hh * dv, (hh + 1) * dv)
        qr = rot(q_ref[:, ks].astype(F32))
        kr = rot(k_ref[:, ks].astype(F32)) * (dk ** -0.5)
        if valid < c:
            kr = jnp.where(krow < valid, kr, 0.0)
        v = v_ref[:, vs]

        lg = lg_ref[hh]
        q_dec = jnp.exp(lg * (ti + 1.0))
        k_dec = jnp.exp(lg * (valid - 1.0 - ti))
        q_dec = jnp.concatenate([q_dec] * (dk // half), axis=1)
        k_dec = jnp.concatenate([k_dec] * (dk // half), axis=1)
        dmask = jnp.where(rel >= 0.0, jnp.exp(lg[:, :1] * jnp.maximum(rel, 0.0)), 0.0)

        scores = _dot_nt(qr.astype(BF16), kr.astype(BF16)) * dmask
        s_old = s_ref[hh]
        o = _dot(scores.astype(BF16), v) + _dot((qr * q_dec).astype(BF16), s_old.astype(BF16))
        s_dec = jnp.concatenate([jnp.exp(lg * float(valid))] * (dv // LANES), axis=1)
        s_ref[hh] = s_old * s_dec + _dot_tn((kr * k_dec).astype(BF16), v)

        mu = jnp.mean(o, axis=-1, keepdims=True)
        d = o - mu
        var = jnp.mean(d * d, axis=-1, keepdims=True)
        on = d * lax.rsqrt(var + NORM_EPS) * gn_ref[:, vs]
        o_ref[:, vs] = (_silu(g_ref[:, vs].astype(F32)) * on).astype(o_ref.dtype)

    @pl.when(t == pl.num_programs(2) - 1)
    def _():
        sfin_ref[...] = s_ref[...]


def _ret_recurrence(z, pos, g_norm, layer, s0, o_prev, s_prev, *, batch, length, row0, chunk, valid, hps):
    h = RET_HEADS
    n_layers = g_norm.shape[0]
    dv = g_norm.shape[1] // h
    dk = (z.shape[1] - 2 * h * dv) // (2 * h)
    half = dk // 2
    nt = length // chunk
    assert length % chunk == 0 and half == LANES and row0 % chunk == 0 and h % hps == 0
    rb0 = row0 // chunk
    hg_n = h // hps
    kcol, vcol, gcol = hg_n, (2 * h * dk) // (hps * dv), (2 * h * dk) // (hps * dv) + hg_n

    inv_freq = RET_THETA ** (-jnp.arange(half, dtype=F32) / half)
    ang = pos.astype(F32)[:, None] * inv_freq[None, :]
    cos, sin = jnp.cos(ang), jnp.sin(ang)
    log_gamma = jnp.log1p(-jnp.exp2(-5.0 - jnp.arange(h, dtype=F32)))
    lg = jnp.broadcast_to(log_gamma[:, None, None], (h, 1, LANES))

    def rowblk(b, t):
        return rb0 + b * nt + t

    in_specs = [
        ((chunk, hps * dk), lambda b, hg, t: (rowblk(b, t), hg)),
        ((chunk, hps * dk), lambda b, hg, t: (rowblk(b, t), kcol + hg)),
        ((chunk, hps * dv), lambda b, hg, t: (rowblk(b, t), vcol + hg)),
        ((chunk, hps * dv), lambda b, hg, t: (rowblk(b, t), gcol + hg)),
        ((chunk, half), lambda b, hg, t: (t, 0)),
        ((chunk, half), lambda b, hg, t: (t, 0)),
        ((hps, 1, LANES), lambda b, hg, t: (hg, 0, 0)),
        ((None, 1, hps * dv), lambda b, hg, t: (layer, 0, hg)),
    ]
    args = [z, z, z, z, cos, sin, lg, g_norm.reshape(n_layers, 1, -1)]
    n_alias = (o_prev is not None) + (s_prev is not None)
    kern = functools.partial(_ret_kernel, valid=valid, hps=hps, has_s0=s0 is not None, n_alias=n_alias)
    return _mixer_call(kern, "retention_recurrence", "hbt", batch=batch, heads=h, hps=hps, nt=nt, rows=chunk, dk=dk, dv=dv,
                       row0=row0, layer=layer, n_layers=n_layers, in_specs=in_specs, args=args, s0=s0,
                       o_prev=o_prev, s_prev=s_prev, m_total=z.shape[0])


def _rope_tables(pos, hd, n_rot):
    half = n_rot // 2
    inv_freq = ROPE_THETA ** (-jnp.arange(half, dtype=F32) / half)
    ang = pos.astype(F32)[:, None] * inv_freq[None, :]
    cos, sin = jnp.cos(ang), jnp.sin(ang)
    n = pos.shape[0]
    ones = jnp.ones((n, hd - n_rot), F32)
    zeros = jnp.zeros((n, hd - n_rot), F32)
    zh = jnp.zeros((n, half), F32)
    a = jnp.concatenate([cos, cos, ones], axis=1)
    b = jnp.concatenate([-sin, zh, zeros], axis=1)
    c = jnp.concatenate([zh, sin, zeros], axis=1)
    rep = LANES // hd
    return jnp.tile(a, (1, rep)), jnp.tile(b, (1, rep)), jnp.tile(c, (1, rep))


def _div_pow2(x, d):
    assert d & (d - 1) == 0
    return x >> (d.bit_length() - 1)


def _mod_pow2(x, d):
    assert d & (d - 1) == 0
    return x & (d - 1)


def _rope128(x, a, b, c, half):
    return x * a + pltpu.roll(x, LANES - half, 1) * b + pltpu.roll(x, half, 1) * c


def _swa_prompt_kernel(sinks_ref, q_ref, k_ref, v_ref, a_ref, b_ref, c_ref, o_ref, krot_ref,
                       qrot_ref, kpad_ref, vext_ref, *, hd, half):
    w = q_ref.shape[0]
    nq = q_ref.shape[1]
    nkv = k_ref.shape[1]
    n_heads, n_kv = nq // hd, nkv // hd
    gpk = (n_heads // n_kv) * hd // LANES
    assert LANES == 2 * hd
    i = pl.program_id(1)
    low = lax.broadcasted_iota(jnp.int32, (w, LANES), 1) < hd
    slot = i & 1
    cur = pl.ds(pl.multiple_of(slot * w, w), w)

    @pl.when(i == 0)
    def _():
        for kv in range(n_kv):
            for par in range(2):
                kpad_ref[kv, par] = jnp.zeros((2 * w, LANES), BF16)
        l2 = lax.broadcasted_iota(jnp.int32, (2 * w, 2 * LANES), 1)
        ones_even = jnp.where((l2 >= LANES) & (l2 < LANES + hd), 1.0, 0.0).astype(BF16)
        ones_odd = jnp.where(l2 >= LANES + hd, 1.0, 0.0).astype(BF16)
        for kv in range(n_kv):
            vext_ref[kv, 0] = ones_even
            vext_ref[kv, 1] = ones_odd

    a, b, c = a_ref[...], b_ref[...], c_ref[...]
    scale = hd ** -0.5
    ri = lax.broadcasted_iota(jnp.int32, (LANES, LANES), 0)
    cj = lax.broadcasted_iota(jnp.int32, (LANES, LANES), 1)
    jm = cj & (hd - 1)
    perm = jnp.where(((jm < half) & (ri == cj + half)) | ((jm >= half) & (jm < 2 * half) & (ri == cj - half)), 1.0, 0.0)
    perm = perm.astype(BF16)
    sgn = b + c
    for m in range(nq // LANES):
        sl = slice(m * LANES, (m + 1) * LANES)
        x = q_ref[:, sl]
        qrot_ref[:, sl] = ((x * a + _dot(x.astype(BF16), perm) * sgn) * scale).astype(BF16)
    for m in range(nkv // LANES):
        sl = slice(m * LANES, (m + 1) * LANES)
        kr = _rope128(k_ref[:, sl], a, b, c, half)
        krot_ref[:, sl] = kr
        vr = v_ref[:, sl]
        k_sw = pltpu.roll(kr, hd, 1)
        v_sw = pltpu.roll(vr, hd, 1)
        for u in range(2):
            kv = 2 * m + u
            k_lo, k_hi = (kr, k_sw) if u == 0 else (k_sw, kr)
            v_lo, v_hi = (vr, v_sw) if u == 0 else (v_sw, vr)
            kpad_ref[kv, 0, cur, :] = jnp.where(low, k_lo, 0.0).astype(BF16)
            kpad_ref[kv, 1, cur, :] = jnp.where(low, 0.0, k_hi).astype(BF16)
            vext_ref[kv, 0, cur, 0:LANES] = jnp.where(low, v_lo, 0.0).astype(BF16)
            vext_ref[kv, 1, cur, 0:LANES] = jnp.where(low, 0.0, v_hi).astype(BF16)

    qi = lax.broadcasted_iota(jnp.int32, (w, 2 * w), 0)
    kj = lax.broadcasted_iota(jnp.int32, (w, 2 * w), 1)
    is_cur = jnp.where(kj >= w, 1, 0) == slot
    rel = qi - (kj & (w - 1)) + jnp.where(is_cur, 0, w)
    prev_ok = jnp.where(i > 0, 1, 0)
    mask = (rel >= 0) & (rel <= w) & ((jnp.where(is_cur, 1, 0) + prev_ok) > 0)
    bias = jnp.where(mask, 0.0, -jnp.inf)

    scores = []
    for kv in range(n_kv):
        qs = jnp.concatenate([qrot_ref[:, (kv * gpk + g) * LANES:(kv * gpk + g + 1) * LANES] for g in range(gpk)], axis=0)
        scores.append([_dot_nt(qs, kpad_ref[kv, par]) for par in range(2)])
    probs, ms = [], {}
    for kv in range(n_kv):
        slabs = ([], [])
        for g in range(gpk):
            for par in range(2):
                hq = 2 * (kv * gpk + g) + par
                sink = sinks_ref[hq]
                s = scores[kv][par][g * w:(g + 1) * w, :] + bias
                m = jnp.maximum(jnp.max(s, axis=-1, keepdims=True), sink)
                slabs[par].append(jnp.exp(s - m).astype(BF16))
                ms[hq] = jnp.exp(sink - m)
        probs.append([jnp.concatenate(sl, axis=0) for sl in slabs])
    res = [_dot(probs[kv][0], vext_ref[kv, 0]) + _dot(probs[kv][1], vext_ref[kv, 1]) for kv in range(n_kv)]
    for kv in range(n_kv):
        for g in range(gpk):
            mq = kv * gpk + g
            blk = res[kv][g * w:(g + 1) * w, :]
            denom = blk[:, LANES:] + jnp.where(low, ms[2 * mq], ms[2 * mq + 1])
            o_ref[:, mq * LANES:(mq + 1) * LANES] = (blk[:, :LANES] / denom).astype(o_ref.dtype)


def _swa_prompt(z, sinks, *, batch, length, hd, n_rot):
    w = SWA_WINDOW
    nq, nkv = SWA_HEADS * hd, SWA_KV_HEADS * hd
    nb = length // w
    assert length % w == 0 and nq % nkv == 0 and LANES % hd == 0
    a, b, c = _rope_tables(jnp.arange(length, dtype=jnp.int32), hd, n_rot)
    kcol = nq // nkv
    tab = pl.BlockSpec((w, LANES), lambda bb, i: (i, 0))
    kern = functools.partial(_swa_prompt_kernel, hd=hd, half=n_rot // 2)
    return pl.pallas_call(
        kern,
        grid=(batch, nb),
        in_specs=[
            pl.BlockSpec(memory_space=pltpu.SMEM),
            pl.BlockSpec((w, nq), lambda bb, i: (bb * nb + i, 0)),
            pl.BlockSpec((w, nkv), lambda bb, i: (bb * nb + i, kcol)),
            pl.BlockSpec((w, nkv), lambda bb, i: (bb * nb + i, kcol + 1)),
            tab, tab, tab,
        ],
        out_specs=[
            pl.BlockSpec((w, nq), lambda bb, i: (bb * nb + i, 0)),
            pl.BlockSpec((None, w, nkv), lambda bb, i: (bb, 0, 0)),
        ],
        out_shape=[
            jax.ShapeDtypeStruct((z.shape[0], nq), BF16),
            jax.ShapeDtypeStruct((batch, w, nkv), F32),
        ],
        scratch_shapes=[pltpu.VMEM((w, nq), BF16), pltpu.VMEM((SWA_KV_HEADS, 2, 2 * w, LANES), BF16),
                        pltpu.VMEM((SWA_KV_HEADS, 2, 2 * w, 2 * LANES), BF16)],
        compiler_params=_params("parallel", "arbitrary"),
        name="swa_prompt",
    )(sinks, z, z, z, a, b, c)


def _swa_sample_kernel(q_ref, k_ref, v_ref, ck_ref, cv_ref, a_ref, b_ref, c_ref, sink_ref, o_prev_ref, o_ref, krot_ref,
                       qall_ref, kall_ref, vall_ref, knew_ref, vnew_ref, *, hd, half, valid):
    del o_prev_ref
    rows = q_ref.shape[0]
    w = ck_ref.shape[0]
    nq = q_ref.shape[1]
    nkv = k_ref.shape[1]
    n_heads, n_kv = nq // hd, nkv // hd
    group = n_heads // n_kv
    per = LANES // hd

    a, b, c = a_ref[...], b_ref[...], c_ref[...]
    scale = hd ** -0.5
    for m in range(nq // LANES):
        r = _rope128(q_ref[:, m * LANES:(m + 1) * LANES], a, b, c, half) * scale
        for u in range(per):
            hq = m * per + u
            qall_ref[hq * rows:(hq + 1) * rows, :] = r[:, u * hd:(u + 1) * hd]
    for m in range(nkv // LANES):
        sl = slice(m * LANES, (m + 1) * LANES)
        kr = _rope128(k_ref[:, sl], a, b, c, half)
        krot_ref[:, sl] = kr
        for u in range(per):
            kv = m * per + u
            knew_ref[kv * rows:(kv + 1) * rows, :] = kr[:, u * hd:(u + 1) * hd]
    for kv in range(n_kv):
        ksl = slice(kv * hd, (kv + 1) * hd)
        kall_ref[kv * w:(kv + 1) * w, :] = ck_ref[:, ksl].astype(BF16)
        vall_ref[kv * w:(kv + 1) * w, :] = cv_ref[:, ksl].astype(BF16)
        vnew_ref[kv * rows:(kv + 1) * rows, :] = v_ref[:, ksl]

    nrow = n_heads * rows
    qa = qall_ref[...].astype(BF16)
    sc = _dot_nt(qa, kall_ref[...])
    sn = _dot_nt(qa, knew_ref[...].astype(BF16))
    rc = lax.broadcasted_iota(jnp.int32, (nrow, n_kv * w), 0)
    cc = lax.broadcasted_iota(jnp.int32, (nrow, n_kv * w), 1)
    mask_c = (_div_pow2(cc, w) == _div_pow2(rc, group * rows)) & (_mod_pow2(cc, w) >= _mod_pow2(rc, rows))
    rn = lax.broadcasted_iota(jnp.int32, (nrow, n_kv * rows), 0)
    cn = lax.broadcasted_iota(jnp.int32, (nrow, n_kv * rows), 1)
    jn = _mod_pow2(cn, rows)
    mask_n = (_div_pow2(cn, rows) == _div_pow2(rn, group * rows)) & (jn <= _mod_pow2(rn, rows)) & (jn < valid)
    sc = jnp.where(mask_c, sc, -jnp.inf)
    sn = jnp.where(mask_n, sn, -jnp.inf)
    sink = sink_ref[...]
    m = jnp.maximum(jnp.maximum(jnp.max(sc, axis=-1, keepdims=True), jnp.max(sn, axis=-1, keepdims=True)), sink)
    pc = jnp.exp(sc - m)
    pn = jnp.exp(sn - m)
    denom = jnp.sum(pc, axis=-1, keepdims=True) + jnp.sum(pn, axis=-1, keepdims=True) + jnp.exp(sink - m)
    o = (_dot(pc.astype(BF16), vall_ref[...]) + _dot(pn.astype(BF16), vnew_ref[...].astype(BF16))) / denom
    for hq in range(n_heads):
        o_ref[:, hq * hd:(hq + 1) * hd] = o[hq * rows:(hq + 1) * rows, :].astype(o_ref.dtype)


def _swa_sample(z, ck, cv, sinks, pos, o_prev, *, batch, row0, rows, valid, hd, n_rot):
    nq, nkv = SWA_HEADS * hd, SWA_KV_HEADS * hd
    w = ck.shape[1]
    a, b, c = _rope_tables(pos, hd, n_rot)
    kcol = nq // nkv
    rb0 = row0 // rows
    nrow = SWA_HEADS * rows
    sink_col = jnp.repeat(sinks, rows)[:, None]
    tab = pl.BlockSpec((rows, LANES), lambda bb: (0, 0))
    kern = functools.partial(_swa_sample_kernel, hd=hd, half=n_rot // 2, valid=valid)
    return pl.pallas_call(
        kern,
        grid=(batch,),
        in_specs=[
            pl.BlockSpec((rows, nq), lambda bb: (rb0 + bb, 0)),
            pl.BlockSpec((rows, nkv), lambda bb: (rb0 + bb, kcol)),
            pl.BlockSpec((rows, nkv), lambda bb: (rb0 + bb, kcol + 1)),
            pl.BlockSpec((None, w, nkv), lambda bb: (bb, 0, 0)),
            pl.BlockSpec((None, w, nkv), lambda bb: (bb, 0, 0)),
            tab, tab, tab,
            pl.BlockSpec((nrow, 1), lambda bb: (0, 0)),
            pl.BlockSpec(memory_space=pl.ANY),
        ],
        out_specs=[
            pl.BlockSpec((rows, nq), lambda bb: (rb0 + bb, 0)),
            pl.BlockSpec((None, rows, nkv), lambda bb: (bb, 0, 0)),
        ],
        out_shape=[
            jax.ShapeDtypeStruct((z.shape[0], nq), BF16),
            jax.ShapeDtypeStruct((batch, rows, nkv), F32),
        ],
        scratch_shapes=[pltpu.VMEM((nrow, hd), F32), pltpu.VMEM((SWA_KV_HEADS * w, hd), BF16),
                        pltpu.VMEM((SWA_KV_HEADS * w, hd), BF16), pltpu.VMEM((SWA_KV_HEADS * rows, hd), F32),
                        pltpu.VMEM((SWA_KV_HEADS * rows, hd), F32)],
        input_output_aliases={9: 0},
        compiler_params=_params("parallel"),
        name="swa_sample",
    )(z, z, z, ck, cv, a, b, c, sink_col, o_prev)


def _merge_rows(xp, xs, rows):
    n = xp.shape[-1]
    xs = jnp.pad(xs, ((0, 0), (0, rows - xs.shape[1]), (0, 0)))
    return jnp.concatenate([xp.reshape(-1, n), xs.reshape(-1, n)], axis=0)


def kernel(x_prompt, x_sample, state_gla, state_ret, cache_swa_k, cache_swa_v, p_prompt, p_sample, norm_mix, norm_mlp, norm_ple, norm_final, gla_w_in, gla_w_a2, gla_b_a, gla_norm, gla_w_out, ret_w_in, ret_norm, ret_w_out, swa_w_in, swa_b_in, swa_sinks, swa_w_out, swa_b_out, mlp_w_up, mlp_w_down, ple_w_gate, ple_w_proj):
    bp, lp, d = x_prompt.shape
    bs, ls, _ = x_sample.shape
    depth = norm_mix.shape[0]
    lpad = SUBLANES
    mp, ms = bp * lp, bs * lpad
    pos_p = jnp.arange(lp, dtype=jnp.int32)
    pos_s = PAST_LEN + jnp.arange(lpad, dtype=jnp.int32)

    x = _merge_rows(x_prompt, x_sample, lpad)
    p_all = jnp.stack([_merge_rows(p_prompt[i], p_sample[i], lpad) for i in range(depth)])
    gla_w_in_t = jnp.swapaxes(gla_w_in, 1, 2)

    gla_p = gla_s = ret_p = ret_s = None
    swa_kp, swa_vp, swa_ks, swa_vs = [], [], [], []
    h, ssq = _rmsnorm(x, norm_mix[0], BF16), None
    for i in range(depth):
        j = i // N_MIXERS
        if i % N_MIXERS == 0:
            rank = gla_w_a2.shape[1]
            nz = gla_w_in.shape[2] - rank
            z = _matmul(h, gla_w_in_t, j, n=nz, w_t=True, row_ssq=ssq, out_dtype=BF16, name="gla_in")
            alow = _matmul(h, gla_w_in_t, j, n=rank, col0=nz, w_t=True, row_ssq=ssq, out_dtype=F32, name="gla_in_lowrank")
            o, gla_p = _gla_recurrence(z, alow, gla_w_a2, gla_b_a, gla_norm, j, None, None, gla_p, batch=bp, length=lp,
                                       row0=0, rows=min(GLA_ROWS, lp), chunk=GLA_CHUNK, sub=GLA_SUB, valid=GLA_CHUNK)
            o, gla_s = _gla_recurrence(z, alow, gla_w_a2, gla_b_a, gla_norm, j, state_gla, o, gla_s, batch=bs,
                                       length=lpad, row0=mp, rows=lpad, chunk=lpad, sub=lpad, valid=ls)
            x, h, ssq = _matmul(o, gla_w_out, j, resid=x, norm_out=norm_mlp[i], out_dtype=F32, name="gla_out")
        elif i % N_MIXERS == 1:
            z = _matmul(h, ret_w_in, j, row_ssq=ssq, out_dtype=BF16, name="ret_in")
            cp = min(RET_CHUNK, lp)
            o, ret_p = _ret_recurrence(z, pos_p, ret_norm, j, None, None, ret_p, batch=bp, length=lp, row0=0,
                                       chunk=cp, valid=cp, hps=1)
            o, ret_s = _ret_recurrence(z, pos_s, ret_norm, j, state_ret, o, ret_s, batch=bs, length=lpad, row0=mp,
                                       chunk=lpad, valid=ls, hps=RET_SAMPLE_HEADS_PER_STEP)
            x, h, ssq = _matmul(o, ret_w_out, j, resid=x, norm_out=norm_mlp[i], out_dtype=F32, name="ret_out")
        else:
            hd = d // SWA_HEADS
            n_rot = hd // 4
            nq, nkv = SWA_HEADS * hd, SWA_KV_HEADS * hd
            z = _matmul(h, swa_w_in, j, row_ssq=ssq, bias=swa_b_in, out_dtype=F32, name="swa_in")
            o, k_p = _swa_prompt(z, swa_sinks[j], batch=bp, length=lp, hd=hd, n_rot=n_rot)
            ck = cache_swa_k[j].reshape(bs, -1, nkv)
            cv = cache_swa_v[j].reshape(bs, -1, nkv)
            o, k_s = _swa_sample(z, ck, cv, swa_sinks[j], pos_s, o, batch=bs, row0=mp, rows=lpad, valid=ls, hd=hd, n_rot=n_rot)
            v_new = z[:, nq + nkv:]
            swa_kp.append(k_p.reshape(bp, SWA_WINDOW, SWA_KV_HEADS, hd))
            swa_vp.append(v_new[:mp].reshape(bp, lp, SWA_KV_HEADS, hd)[:, -SWA_WINDOW:])
            swa_ks.append(k_s[:, :ls].reshape(bs, ls, SWA_KV_HEADS, hd))
            swa_vs.append(v_new[mp:].reshape(bs, lpad, SWA_KV_HEADS, hd)[:, :ls])
            x, h, ssq = _matmul(o, swa_w_out, j, bias=swa_b_out, resid=x, norm_out=norm_mlp[i], out_dtype=F32,
                                name="swa_out")
        u = _matmul(h, mlp_w_up, i, row_ssq=ssq, act="relu2", out_dtype=BF16, name="mlp_up")
        x, h, ssq = _matmul(u, mlp_w_down, i, resid=x, norm_out=norm_ple[i], out_dtype=F32, name="mlp_down")
        if i + 1 < depth:
            x, h, ssq = _matmul(h, ple_w_gate, i, row_ssq=ssq, gate=(p_all[i], ple_w_proj), resid=x,
                                norm_out=norm_mix[i + 1], out_dtype=F32, name="ple")
        else:
            x = _matmul(h, ple_w_gate, i, row_ssq=ssq, gate=(p_all[i], ple_w_proj), resid=x, out_dtype=F32, name="ple")
    y_p = _rmsnorm(x, norm_final, F32, row0=0, nrows=mp)
    y_s = _rmsnorm(x, norm_final, F32, row0=mp, nrows=ms)
    return (y_p.reshape(bp, lp, d), y_s.reshape(bs, lpad, d)[:, :ls],
            gla_p, gla_s, ret_p, ret_s,
            jnp.stack(swa_kp), jnp.stack(swa_vp), jnp.stack(swa_ks), jnp.stack(swa_vs))
```

```python
import functools

import jax
import jax.numpy as jnp
from jax import lax
from jax.experimental import pallas as pl
from jax.experimental.pallas import tpu as pltpu

F32 = jnp.float32
BF16 = jnp.bfloat16

VMEM_LIMIT_BYTES = 56 * 1024 * 1024
LANES = 128
SUBLANES = 8
MXU_COLS = 256

NORM_EPS = 1e-6
PAST_LEN = 8192
N_MIXERS = 3

GLA_HEADS = 8
GLA_TAU = 16.0
GLA_CHUNK = 64
GLA_SUB = 8
GLA_ROWS = 512
GLA_HEADS_PER_STEP = 2

RET_HEADS = 16
RET_THETA = 10000.0
RET_CHUNK = 256
RET_PROMPT_HEADS_PER_STEP = 2
RET_SAMPLE_HEADS_PER_STEP = 4

SWA_HEADS = 64
SWA_KV_HEADS = 8
SWA_WINDOW = 128
ROPE_THETA = 500000.0

ROW_BLOCK = 1408
COL_BLOCK = 512
ROW_CHUNKS = 2
DEEP_ROW_BLOCK = 2112
DEEP_COL_BLOCK = 512
DEEP_K_BLOCK = 2048
NORM_ROWS = 512


def _params(*sem):
    return pltpu.CompilerParams(dimension_semantics=sem, vmem_limit_bytes=VMEM_LIMIT_BYTES)


def _row_block(m, target):
    if m <= target:
        return m
    for d in range(target, SUBLANES - 1, -1):
        if m % d == 0 and d % SUBLANES == 0:
            return d
    raise ValueError(f"no row block for {m}")


def _dot(a, b):
    return jnp.dot(a, b, preferred_element_type=F32)


def _dot_nt(a, b):
    return lax.dot_general(a, b, (((1,), (1,)), ((), ())), preferred_element_type=F32)


def _dot_tn(a, b):
    return lax.dot_general(a, b, (((0,), (0,)), ((), ())), preferred_element_type=F32)


def _silu(x):
    return x * jax.nn.sigmoid(x)


def _rmsnorm_kernel(x_ref, g_ref, o_ref):
    x = x_ref[...]
    ms = jnp.mean(x * x, axis=-1, keepdims=True)
    o_ref[...] = (x * lax.rsqrt(ms + NORM_EPS) * g_ref[...]).astype(o_ref.dtype)


def _rmsnorm(x, g, out_dtype, row0=0, nrows=None):
    d = x.shape[1]
    nrows = x.shape[0] if nrows is None else nrows
    bm = _row_block(nrows, NORM_ROWS)
    assert row0 % bm == 0
    off = row0 // bm
    return pl.pallas_call(
        _rmsnorm_kernel,
        grid=(nrows // bm,),
        in_specs=[pl.BlockSpec((bm, d), lambda i: (off + i, 0)), pl.BlockSpec((1, d), lambda i: (0, 0))],
        out_specs=pl.BlockSpec((bm, d), lambda i: (i, 0)),
        out_shape=jax.ShapeDtypeStruct((nrows, d), out_dtype),
        compiler_params=_params("parallel"),
        name="rmsnorm",
    )(x, g.reshape(1, d))


def _matmul_kernel(*refs, nk, nrow, has_bias, has_resid, has_gate, has_scale, has_norm_out, act, w_t):
    it = iter(refs)
    x_ref = next(it)
    w_ref = next(it)
    b_ref = next(it) if has_bias else None
    r_ref = next(it) if has_resid else None
    p_ref = next(it) if has_gate else None
    wp_ref = next(it) if has_gate else None
    ss_ref = next(it) if has_scale else None
    gn_ref = next(it) if has_norm_out else None
    o_ref = next(it)
    xg_ref = next(it) if has_norm_out else None
    ssq_ref = next(it) if has_norm_out else None

    bm = o_ref.shape[0]
    rh = bm // nrow

    def emit_norm_out(val, rs):
        xg_ref[rs, :] = (val * gn_ref[...]).astype(BF16)
        sq = jnp.sum(val * val, axis=-1, keepdims=True)
        first = pl.program_id(1) == 0

        @pl.when(first)
        def _():
            ssq_ref[rs, :] = sq

        @pl.when(jnp.logical_not(first))
        def _():
            ssq_ref[rs, :] += sq

    if nk > 1:
        @pl.when(pl.program_id(2) == 0)
        def _():
            acc = r_ref[...]
            if has_bias:
                acc = acc + b_ref[...]
            o_ref[...] = acc

    wb = w_ref[...].astype(BF16)
    for c in range(nrow):
        rs = slice(c * rh, (c + 1) * rh)
        part = _dot_nt(x_ref[rs, :], wb) if w_t else _dot(x_ref[rs, :], wb)

        if nk == 1:
            acc = part
            if has_scale:
                acc = acc * lax.rsqrt(ss_ref[rs, :] * (1.0 / x_ref.shape[1]) + NORM_EPS)
            if has_bias:
                acc = acc + b_ref[...]
            if act == "relu2":
                r = jnp.maximum(acc, 0.0)
                acc = r * r
            if has_gate:
                proj = _dot(p_ref[rs, :].astype(BF16), wp_ref[...].astype(BF16))
                acc = jax.nn.sigmoid(acc) * proj
            if has_resid:
                acc = r_ref[rs, :] + acc
            o_ref[rs, :] = acc.astype(o_ref.dtype)
            if has_norm_out:
                emit_norm_out(acc, rs)
        else:
            o_ref[rs, :] += part
            if has_norm_out:
                @pl.when(pl.program_id(2) == nk - 1)
                def _():
                    emit_norm_out(o_ref[rs, :], rs)


def _matmul(x, w, layer, *, n=None, col0=0, w_t=False, row_ssq=None, bias=None, resid=None, gate=None, act=None,
            norm_out=None, out_dtype, name):
    m, kdim = x.shape
    n_total = w.shape[1] if w_t else w.shape[2]
    n = n_total if n is None else n
    if kdim <= DEEP_K_BLOCK * 2:
        bm, bk, bn = _row_block(m, ROW_BLOCK), kdim, min(n, COL_BLOCK)
        nrow = ROW_CHUNKS if bm % (ROW_CHUNKS * 2 * SUBLANES) == 0 and norm_out is None else 1
    else:
        bm, bk, bn = _row_block(m, DEEP_ROW_BLOCK), DEEP_K_BLOCK, min(n, DEEP_COL_BLOCK)
        nrow = 1
        assert act is None and gate is None and out_dtype == F32 and resid is not None and row_ssq is None
    assert n % bn == 0 and kdim % bk == 0 and m % bm == 0 and col0 % bn == 0
    nk = kdim // bk
    c0 = col0 // bn
    grid = (m // bm, n // bn, nk)

    if nk == 1:
        x_spec = pl.BlockSpec((bm, bk), lambda i, j, k: (i, k), pipeline_mode=pl.Buffered(1))
    else:
        x_spec = pl.BlockSpec((bm, bk), lambda i, j, k: (i, k))
    if w_t:
        w_spec = pl.BlockSpec((None, bn, bk), lambda i, j, k: (layer, c0 + j, k))
    else:
        w_spec = pl.BlockSpec((None, bk, bn), lambda i, j, k: (layer, k, c0 + j))
    in_specs = [x_spec, w_spec]
    args = [x, w]
    if bias is not None:
        in_specs.append(pl.BlockSpec((None, 1, bn), lambda i, j, k: (layer, 0, j)))
        args.append(bias.reshape(bias.shape[0], 1, bias.shape[1]))
    if resid is not None:
        in_specs.append(pl.BlockSpec((bm, bn), lambda i, j, k: (i, j)))
        args.append(resid)
    if gate is not None:
        p, wp = gate
        kp = p.shape[1]
        in_specs.append(pl.BlockSpec((bm, kp), lambda i, j, k: (i, 0)))
        in_specs.append(pl.BlockSpec((None, kp, bn), lambda i, j, k: (layer, 0, j)))
        args += [p, wp]
    if row_ssq is not None:
        in_specs.append(pl.BlockSpec((bm, 1), lambda i, j, k: (i, 0)))
        args.append(row_ssq)
    out_specs = [pl.BlockSpec((bm, bn), lambda i, j, k: (i, j))]
    out_shape = [jax.ShapeDtypeStruct((m, n), out_dtype)]
    if norm_out is not None:
        assert out_dtype == F32
        in_specs.append(pl.BlockSpec((1, bn), lambda i, j, k: (0, j)))
        args.append(norm_out.reshape(1, n))
        out_specs += [pl.BlockSpec((bm, bn), lambda i, j, k: (i, j)), pl.BlockSpec((bm, 1), lambda i, j, k: (i, 0))]
        out_shape += [jax.ShapeDtypeStruct((m, n), BF16), jax.ShapeDtypeStruct((m, 1), F32)]
    kern = functools.partial(_matmul_kernel, nk=nk, nrow=nrow, has_bias=bias is not None, has_resid=resid is not None,
                             has_gate=gate is not None, has_scale=row_ssq is not None,
                             has_norm_out=norm_out is not None, act=act, w_t=w_t)
    col_sem = "arbitrary" if norm_out is not None else "parallel"
    outs = pl.pallas_call(
        kern,
        grid=grid,
        in_specs=in_specs,
        out_specs=out_specs,
        out_shape=out_shape,
        compiler_params=_params("parallel", col_sem, "arbitrary"),
        name=name,
    )(*args)
    return outs if norm_out is not None else outs[0]


def _mixer_call(kern, name, grid_order, *, batch, heads, hps, nt, rows, dk, dv, row0, layer, n_layers,
                in_specs, args, s0, o_prev, s_prev, m_total, extra_scratch=()):
    rb0 = row0 // rows
    assert row0 % rows == 0 and heads % hps == 0

    def ix(f):
        if grid_order == "bht":
            return lambda b, hg, t: f(b, hg, t)
        return lambda hg, b, t: f(b, hg, t)

    in_specs = [pl.BlockSpec(bs, ix(f)) for bs, f in in_specs]
    args = list(args)
    aliases = {}
    if s0 is not None:
        in_specs.append(pl.BlockSpec((None, None, hps, dk, dv), ix(lambda b, hg, t: (layer, b, hg, 0, 0))))
        args.append(s0)
    if o_prev is not None:
        aliases[len(args)] = 0
        in_specs.append(pl.BlockSpec(memory_space=pl.ANY))
        args.append(o_prev)
    if s_prev is not None:
        aliases[len(args)] = 1
        in_specs.append(pl.BlockSpec(memory_space=pl.ANY))
        args.append(s_prev)
    grid = (batch, heads // hps, nt) if grid_order == "bht" else (heads // hps, batch, nt)
    return pl.pallas_call(
        kern,
        grid=grid,
        in_specs=in_specs,
        out_specs=[
            pl.BlockSpec((rows, hps * dv), ix(lambda b, hg, t: (rb0 + b * nt + t, hg))),
            pl.BlockSpec((None, None, hps, dk, dv), ix(lambda b, hg, t: (layer, b, hg, 0, 0))),
        ],
        out_shape=[
            jax.ShapeDtypeStruct((m_total, heads * dv), BF16),
            jax.ShapeDtypeStruct((n_layers, batch, heads, dk, dv), F32),
        ],
        scratch_shapes=[pltpu.VMEM((hps, dk, dv), F32), *extra_scratch],
        input_output_aliases=aliases,
        compiler_params=_params("parallel", "parallel", "arbitrary"),
        name=name,
    )(*args)


def _split3(x):
    hi = x.astype(BF16)
    r1 = x - hi.astype(F32)
    mid = r1.astype(BF16)
    lo = (r1 - mid.astype(F32)).astype(BF16)
    return hi, mid, lo


def _gla_kernel(*refs, rows, chunk, sub, valid, hps, has_s0, n_alias):
    it = iter(refs)
    q_ref, k_ref, v_ref, g_ref, al_ref, wa_ref, ba_ref, gn_ref = (next(it) for _ in range(8))
    s0_ref = next(it) if has_s0 else None
    for _ in range(n_alias):
        next(it)
    o_ref, sfin_ref, s_ref = next(it), next(it), next(it)

    dk = q_ref.shape[1] // hps
    dv = v_ref.shape[1] // hps
    t = pl.program_id(2)

    @pl.when(t == 0)
    def _():
        if has_s0:
            s_ref[...] = s0_ref[...]
        else:
            s_ref[...] = jnp.zeros_like(s_ref)

    row_c = lax.broadcasted_iota(jnp.int32, (chunk, chunk), 0)
    col_c = lax.broadcasted_iota(jnp.int32, (chunk, chunk), 1)
    tri = (row_c >= col_c).astype(BF16)
    ones = jnp.ones((chunk, LANES), BF16)
    lane = lax.broadcasted_iota(jnp.int32, (sub, chunk), 1)
    subq = lax.broadcasted_iota(jnp.int32, (sub, chunk), 0)
    chunkrow = lax.broadcasted_iota(jnp.int32, (chunk, dk), 0)

    def head_chunk(rs, hh):
        ks = slice(hh * dk, (hh + 1) * dk)
        vs = slice(hh * dv, (hh + 1) * dv)
        q = q_ref[rs, ks].astype(F32) * (dk ** -0.5)
        k = k_ref[rs, ks].astype(F32)
        v = v_ref[rs, vs]
        x = _dot(al_ref[rs, :], wa_ref[:, ks]) + ba_ref[:, ks]
        la = (jnp.minimum(x, 0.0) - jnp.log1p(jnp.exp(-jnp.abs(x)))) * (1.0 / GLA_TAU)
        if valid < chunk:
            la = jnp.where(chunkrow < valid, la, 0.0)
            k = jnp.where(chunkrow < valid, k, 0.0)
        parts = _split3(la)
        b = sum(_dot(tri, p) for p in parts)
        b_tot = sum(_dot_tn(p, ones) for p in parts)

        s_old = s_ref[hh]
        o = _dot((q * jnp.exp(b)).astype(BF16), s_old.astype(BF16))

        blocks = []
        for i in range(chunk // sub):
            lo_r = i * sub
            b_r = b[lo_r:lo_r + sub, :]
            q_r = q[lo_r:lo_r + sub, :]
            acc = jnp.zeros((sub, chunk), F32)
            if i > 0:
                ref = b[lo_r - 1:lo_r, :]
                qt = (q_r * jnp.exp(b_r - ref)).astype(BF16)
                kt = k[:lo_r, :] * jnp.exp(ref - b[:lo_r, :])
                kt = jnp.concatenate([kt, jnp.zeros((chunk - lo_r, dk), F32)], axis=0)
                acc = _dot_nt(qt, kt.astype(BF16))
            for s in range(sub):
                row = lo_r + s
                e = jnp.exp(b_r - b[row:row + 1, :])
                col = jnp.sum(q_r * k[row:row + 1, :] * e, axis=-1, keepdims=True)
                acc = jnp.where((lane == row) & (subq >= s), col, acc)
            blocks.append(acc)
        scores = blocks[0] if len(blocks) == 1 else jnp.concatenate(blocks, axis=0)
        o = o + _dot(scores.astype(BF16), v)

        kd = (k * jnp.exp(b[chunk - 1:chunk, :] - b)).astype(BF16)
        dec = jnp.exp(b_tot)
        dec = jnp.concatenate([dec] * (dv // LANES), axis=1)
        s_ref[hh] = s_old * dec + _dot_tn(kd, v)

        ms = jnp.mean(o * o, axis=-1, keepdims=True)
        on = o * lax.rsqrt(ms + NORM_EPS) * gn_ref[...]
        o_ref[rs, vs] = (_silu(g_ref[rs, vs].astype(F32)) * on).astype(o_ref.dtype)

    def chunk_body(ci, carry):
        rs = pl.ds(pl.multiple_of(ci * chunk, chunk), chunk)
        for hh in range(hps):
            head_chunk(rs, hh)
        return carry

    if rows == chunk:
        for hh in range(hps):
            head_chunk(slice(0, chunk), hh)
    else:
        lax.fori_loop(0, rows // chunk, chunk_body, 0)

    @pl.when(t == pl.num_programs(2) - 1)
    def _():
        sfin_ref[...] = s_ref[...]


def _gla_recurrence(z, alow, w_a2, b_a, g_norm, layer, s0, o_prev, s_prev, *, batch, length, row0, rows, chunk, sub, valid):
    h, hps = GLA_HEADS, GLA_HEADS_PER_STEP
    n_layers = w_a2.shape[0]
    dk = w_a2.shape[2] // h
    dv = g_norm.shape[1]
    nt = length // rows
    assert length % rows == 0 and rows % chunk == 0 and chunk % sub == 0 and row0 % rows == 0
    rb0 = row0 // rows
    hg_n = h // hps
    kcol, vcol, gcol = hg_n, (2 * h * dk) // (hps * dv), (2 * h * dk) // (hps * dv) + hg_n

    def rowblk(b, t):
        return rb0 + b * nt + t

    in_specs = [
        ((rows, hps * dk), lambda b, hg, t: (rowblk(b, t), hg)),
        ((rows, hps * dk), lambda b, hg, t: (rowblk(b, t), kcol + hg)),
        ((rows, hps * dv), lambda b, hg, t: (rowblk(b, t), vcol + hg)),
        ((rows, hps * dv), lambda b, hg, t: (rowblk(b, t), gcol + hg)),
        ((rows, alow.shape[1]), lambda b, hg, t: (rowblk(b, t), 0)),
        ((None, w_a2.shape[1], hps * dk), lambda b, hg, t: (layer, 0, hg)),
        ((None, 1, hps * dk), lambda b, hg, t: (layer, 0, hg)),
        ((None, 1, dv), lambda b, hg, t: (layer, 0, 0)),
    ]
    args = [z, z, z, z, alow, w_a2, b_a.reshape(n_layers, 1, -1), g_norm.reshape(n_layers, 1, -1)]
    n_alias = (o_prev is not None) + (s_prev is not None)
    kern = functools.partial(_gla_kernel, rows=rows, chunk=chunk, sub=sub, valid=valid, hps=hps,
                             has_s0=s0 is not None, n_alias=n_alias)
    return _mixer_call(kern, "gla_recurrence", "bht", batch=batch, heads=h, hps=hps, nt=nt, rows=rows, dk=dk, dv=dv,
                       row0=row0, layer=layer, n_layers=n_layers, in_specs=in_specs, args=args, s0=s0,
                       o_prev=o_prev, s_prev=s_prev, m_total=z.shape[0])


def _ret_kernel(*refs, valid, hps, has_s0, n_alias):
    it = iter(refs)
    q_ref, k_ref, v_ref, g_ref, cos_ref, sin_ref, lg_ref, gn_ref = (next(it) for _ in range(8))
    s0_ref = next(it) if has_s0 else None
    for _ in range(n_alias):
        next(it)
    o_ref, sfin_ref, s_ref = next(it), next(it), next(it)

    c = q_ref.shape[0]
    dk = q_ref.shape[1] // hps
    dv = v_ref.shape[1] // hps
    half = dk // 2
    t = pl.program_id(2)

    @pl.when(t == 0)
    def _():
        if has_s0:
            s_ref[...] = s0_ref[...]
        else:
            s_ref[...] = jnp.zeros_like(s_ref)

    cos = cos_ref[...]
    sin = sin_ref[...]

    def rot(x):
        x1, x2 = x[:, :half], x[:, half:]
        return jnp.concatenate([x1 * cos - x2 * sin, x2 * cos + x1 * sin], axis=1)

    ti = lax.broadcasted_iota(jnp.int32, (c, half), 0).astype(F32)
    rel = (lax.broadcasted_iota(jnp.int32, (c, c), 0) - lax.broadcasted_iota(jnp.int32, (c, c), 1)).astype(F32)
    krow = lax.broadcasted_iota(jnp.int32, (c, dk), 0)

    for hh in range(hps):
        ks = slice(hh * dk, (hh + 1) * dk)
        vs = slice(hh * dv, (hh + 1) * dv)
        qr = rot(q_ref[:, ks].astype(F32))
        kr = rot(k_ref[:, ks].astype(F32)) * (dk ** -0.5)
        if valid < c:
            kr = jnp.where(krow < valid, kr, 0.0)
        v = v_ref[:, vs]

        lg = lg_ref[hh]
        q_dec = jnp.exp(lg * (ti + 1.0))
        k_dec = jnp.exp(lg * (valid - 1.0 - ti))
        q_dec = jnp.concatenate([q_dec] * (dk // half), axis=1)
        k_dec = jnp.concatenate([k_dec] * (dk // half), axis=1)
        dmask = jnp.where(rel >= 0.0, jnp.exp(lg[:, :1] * jnp.maximum(rel, 0.0)), 0.0)

        scores = _dot_nt(qr.astype(BF16), kr.astype(BF16)) * dmask
        s_old = s_ref[hh]
        o = _dot(scores.astype(BF16), v) + _dot((qr * q_dec).astype(BF16), s_old.astype(BF16))
        s_dec = jnp.concatenate([jnp.exp(lg * float(valid))] * (dv // LANES), axis=1)
        s_ref[hh] = s_old * s_dec + _dot_tn((kr * k_dec).astype(BF16), v)

        mu = jnp.mean(o, axis=-1, keepdims=True)
        d = o - mu
        var = jnp.mean(d * d, axis=-1, keepdims=True)
        on = d * lax.rsqrt(var + NORM_EPS) * gn_ref[:, vs]
        o_ref[:, vs] = (_silu(g_ref[:, vs].astype(F32)) * on).astype(o_ref.dtype)

    @pl.when(t == pl.num_programs(2) - 1)
    def _():
        sfin_ref[...] = s_ref[...]


def _ret_recurrence(z, pos, g_norm, layer, s0, o_prev, s_prev, *, batch, length, row0, chunk, valid, hps):
    h = RET_HEADS
    n_layers = g_norm.shape[0]
    dv = g_norm.shape[1] // h
    dk = (z.shape[1] - 2 * h * dv) // (2 * h)
    half = dk // 2
    nt = length // chunk
    assert length % chunk == 0 and half == LANES and row0 % chunk == 0 and h % hps == 0
    rb0 = row0 // chunk
    hg_n = h // hps
    kcol, vcol, gcol = hg_n, (2 * h * dk) // (hps * dv), (2 * h * dk) // (hps * dv) + hg_n

    inv_freq = RET_THETA ** (-jnp.arange(half, dtype=F32) / half)
    ang = pos.astype(F32)[:, None] * inv_freq[None, :]
    cos, sin = jnp.cos(ang), jnp.sin(ang)
    log_gamma = jnp.log1p(-jnp.exp2(-5.0 - jnp.arange(h, dtype=F32)))
    lg = jnp.broadcast_to(log_gamma[:, None, None], (h, 1, LANES))

    def rowblk(b, t):
        return rb0 + b * nt + t

    in_specs = [
        ((chunk, hps * dk), lambda b, hg, t: (rowblk(b, t), hg)),
        ((chunk, hps * dk), lambda b, hg, t: (rowblk(b, t), kcol + hg)),
        ((chunk, hps * dv), lambda b, hg, t: (rowblk(b, t), vcol + hg)),
        ((chunk, hps * dv), lambda b, hg, t: (rowblk(b, t), gcol + hg)),
        ((chunk, half), lambda b, hg, t: (t, 0)),
        ((chunk, half), lambda b, hg, t: (t, 0)),
        ((hps, 1, LANES), lambda b, hg, t: (hg, 0, 0)),
        ((None, 1, hps * dv), lambda b, hg, t: (layer, 0, hg)),
    ]
    args = [z, z, z, z, cos, sin, lg, g_norm.reshape(n_layers, 1, -1)]
    n_alias = (o_prev is not None) + (s_prev is not None)
    kern = functools.partial(_ret_kernel, valid=valid, hps=hps, has_s0=s0 is not None, n_alias=n_alias)
    return _mixer_call(kern, "retention_recurrence", "hbt", batch=batch, heads=h, hps=hps, nt=nt, rows=chunk, dk=dk, dv=dv,
                       row0=row0, layer=layer, n_layers=n_layers, in_specs=in_specs, args=args, s0=s0,
                       o_prev=o_prev, s_prev=s_prev, m_total=z.shape[0])


def _rope_tables(pos, hd, n_rot):
    half = n_rot // 2
    inv_freq = ROPE_THETA ** (-jnp.arange(half, dtype=F32) / half)
    ang = pos.astype(F32)[:, None] * inv_freq[None, :]
    cos, sin = jnp.cos(ang), jnp.sin(ang)
    n = pos.shape[0]
    ones = jnp.ones((n, hd - n_rot), F32)
    zeros = jnp.zeros((n, hd - n_rot), F32)
    zh = jnp.zeros((n, half), F32)
    a = jnp.concatenate([cos, cos, ones], axis=1)
    b = jnp.concatenate([-sin, zh, zeros], axis=1)
    c = jnp.concatenate([zh, sin, zeros], axis=1)
    rep = LANES // hd
    return jnp.tile(a, (1, rep)), jnp.tile(b, (1, rep)), jnp.tile(c, (1, rep))


def _div_pow2(x, d):
    assert d & (d - 1) == 0
    return x >> (d.bit_length() - 1)


def _mod_pow2(x, d):
    assert d & (d - 1) == 0
    return x & (d - 1)


def _rope128(x, a, b, c, half):
    return x * a + pltpu.roll(x, LANES - half, 1) * b + pltpu.roll(x, half, 1) * c


def _swa_prompt_kernel(sinks_ref, q_ref, k_ref, v_ref, a_ref, b_ref, c_ref, o_ref, krot_ref,
                       qrot_ref, kpad_ref, vext_ref, *, hd, half):
    w = q_ref.shape[0]
    nq = q_ref.shape[1]
    nkv = k_ref.shape[1]
    n_heads, n_kv = nq // hd, nkv // hd
    gpk = (n_heads // n_kv) * hd // LANES
    assert LANES == 2 * hd
    i = pl.program_id(1)
    low = lax.broadcasted_iota(jnp.int32, (w, LANES), 1) < hd
    slot = i & 1
    cur = pl.ds(pl.multiple_of(slot * w, w), w)

    @pl.when(i == 0)
    def _():
        for kv in range(n_kv):
            for par in range(2):
                kpad_ref[kv, par] = jnp.zeros((2 * w, LANES), BF16)
        l2 = lax.broadcasted_iota(jnp.int32, (2 * w, 2 * LANES), 1)
        ones_even = jnp.where((l2 >= LANES) & (l2 < LANES + hd), 1.0, 0.0).astype(BF16)
        ones_odd = jnp.where(l2 >= LANES + hd, 1.0, 0.0).astype(BF16)
        for kv in range(n_kv):
            vext_ref[kv, 0] = ones_even
            vext_ref[kv, 1] = ones_odd

    a, b, c = a_ref[...], b_ref[...], c_ref[...]
    scale = hd ** -0.5
    ri = lax.broadcasted_iota(jnp.int32, (LANES, LANES), 0)
    cj = lax.broadcasted_iota(jnp.int32, (LANES, LANES), 1)
    jm = cj & (hd - 1)
    perm = jnp.where(((jm < half) & (ri == cj + half)) | ((jm >= half) & (jm < 2 * half) & (ri == cj - half)), 1.0, 0.0)
    perm = perm.astype(BF16)
    sgn = b + c
    for m in range(nq // LANES):
        sl = slice(m * LANES, (m + 1) * LANES)
        x = q_ref[:, sl]
        qrot_ref[:, sl] = ((x * a + _dot(x.astype(BF16), perm) * sgn) * scale).astype(BF16)
    for m in range(nkv // LANES):
        sl = slice(m * LANES, (m + 1) * LANES)
        kr = _rope128(k_ref[:, sl], a, b, c, half)
        krot_ref[:, sl] = kr
        vr = v_ref[:, sl]
        k_sw = pltpu.roll(kr, hd, 1)
        v_sw = pltpu.roll(vr, hd, 1)
        for u in range(2):
            kv = 2 * m + u
            k_lo, k_hi = (kr, k_sw) if u == 0 else (k_sw, kr)
            v_lo, v_hi = (vr, v_sw) if u == 0 else (v_sw, vr)
            kpad_ref[kv, 0, cur, :] = jnp.where(low, k_lo, 0.0).astype(BF16)
            kpad_ref[kv, 1, cur, :] = jnp.where(low, 0.0, k_hi).astype(BF16)
            vext_ref[kv, 0, cur, 0:LANES] = jnp.where(low, v_lo, 0.0).astype(BF16)
            vext_ref[kv, 1, cur, 0:LANES] = jnp.where(low, 0.0, v_hi).astype(BF16)

    qi = lax.broadcasted_iota(jnp.int32, (w, 2 * w), 0)
    kj = lax.broadcasted_iota(jnp.int32, (w, 2 * w), 1)
    is_cur = jnp.where(kj >= w, 1, 0) == slot
    rel = qi - (kj & (w - 1)) + jnp.where(is_cur, 0, w)
    prev_ok = jnp.where(i > 0, 1, 0)
    mask = (rel >= 0) & (rel <= w) & ((jnp.where(is_cur, 1, 0) + prev_ok) > 0)
    bias = jnp.where(mask, 0.0, -jnp.inf)

    scores = []
    for kv in range(n_kv):
        qs = jnp.concatenate([qrot_ref[:, (kv * gpk + g) * LANES:(kv * gpk + g + 1) * LANES] for g in range(gpk)], axis=0)
        scores.append([_dot_nt(qs, kpad_ref[kv, par]) for par in range(2)])
    probs, ms = [], {}
    for kv in range(n_kv):
        slabs = ([], [])
        for g in range(gpk):
            for par in range(2):
                hq = 2 * (kv * gpk + g) + par
                sink = sinks_ref[hq]
                s = scores[kv][par][g * w:(g + 1) * w, :] + bias
                m = jnp.maximum(jnp.max(s, axis=-1, keepdims=True), sink)
                slabs[par].append(jnp.exp(s - m).astype(BF16))
                ms[hq] = jnp.exp(sink - m)
        probs.append([jnp.concatenate(sl, axis=0) for sl in slabs])
    res = [_dot(probs[kv][0], vext_ref[kv, 0]) + _dot(probs[kv][1], vext_ref[kv, 1]) for kv in range(n_kv)]
    for kv in range(n_kv):
        for g in range(gpk):
            mq = kv * gpk + g
            blk = res[kv][g * w:(g + 1) * w, :]
            denom = blk[:, LANES:] + jnp.where(low, ms[2 * mq], ms[2 * mq + 1])
            o_ref[:, mq * LANES:(mq + 1) * LANES] = (blk[:, :LANES] / denom).astype(o_ref.dtype)


def _swa_prompt(z, sinks, *, batch, length, hd, n_rot):
    w = SWA_WINDOW
    nq, nkv = SWA_HEADS * hd, SWA_KV_HEADS * hd
    nb = length // w
    assert length % w == 0 and nq % nkv == 0 and LANES % hd == 0
    a, b, c = _rope_tables(jnp.arange(length, dtype=jnp.int32), hd, n_rot)
    kcol = nq // nkv
    tab = pl.BlockSpec((w, LANES), lambda bb, i: (i, 0))
    kern = functools.partial(_swa_prompt_kernel, hd=hd, half=n_rot // 2)
    return pl.pallas_call(
        kern,
        grid=(batch, nb),
        in_specs=[
            pl.BlockSpec(memory_space=pltpu.SMEM),
            pl.BlockSpec((w, nq), lambda bb, i: (bb * nb + i, 0)),
            pl.BlockSpec((w, nkv), lambda bb, i: (bb * nb + i, kcol)),
            pl.BlockSpec((w, nkv), lambda bb, i: (bb * nb + i, kcol + 1)),
            tab, tab, tab,
        ],
        out_specs=[
            pl.BlockSpec((w, nq), lambda bb, i: (bb * nb + i, 0)),
            pl.BlockSpec((None, w, nkv), lambda bb, i: (bb, 0, 0)),
        ],
        out_shape=[
            jax.ShapeDtypeStruct((z.shape[0], nq), BF16),
            jax.ShapeDtypeStruct((batch, w, nkv), F32),
        ],
        scratch_shapes=[pltpu.VMEM((w, nq), BF16), pltpu.VMEM((SWA_KV_HEADS, 2, 2 * w, LANES), BF16),
                        pltpu.VMEM((SWA_KV_HEADS, 2, 2 * w, 2 * LANES), BF16)],
        compiler_params=_params("parallel", "arbitrary"),
        name="swa_prompt",
    )(sinks, z, z, z, a, b, c)


def _swa_sample_kernel(q_ref, k_ref, v_ref, ck_ref, cv_ref, a_ref, b_ref, c_ref, sink_ref, o_prev_ref, o_ref, krot_ref,
                       qall_ref, kall_ref, vall_ref, knew_ref, vnew_ref, *, hd, half, valid):
    del o_prev_ref
    rows = q_ref.shape[0]
    w = ck_ref.shape[0]
    nq = q_ref.shape[1]
    nkv = k_ref.shape[1]
    n_heads, n_kv = nq // hd, nkv // hd
    group = n_heads // n_kv
    per = LANES // hd

    a, b, c = a_ref[...], b_ref[...], c_ref[...]
    scale = hd ** -0.5
    for m in range(nq // LANES):
        r = _rope128(q_ref[:, m * LANES:(m + 1) * LANES], a, b, c, half) * scale
        for u in range(per):
            hq = m * per + u
            qall_ref[hq * rows:(hq + 1) * rows, :] = r[:, u * hd:(u + 1) * hd]
    for m in range(nkv // LANES):
        sl = slice(m * LANES, (m + 1) * LANES)
        kr = _rope128(k_ref[:, sl], a, b, c, half)
        krot_ref[:, sl] = kr
        for u in range(per):
            kv = m * per + u
            knew_ref[kv * rows:(kv + 1) * rows, :] = kr[:, u * hd:(u + 1) * hd]
    for kv in range(n_kv):
        ksl = slice(kv * hd, (kv + 1) * hd)
        kall_ref[kv * w:(kv + 1) * w, :] = ck_ref[:, ksl].astype(BF16)
        vall_ref[kv * w:(kv + 1) * w, :] = cv_ref[:, ksl].astype(BF16)
        vnew_ref[kv * rows:(kv + 1) * rows, :] = v_ref[:, ksl]

    nrow = n_heads * rows
    qa = qall_ref[...].astype(BF16)
    sc = _dot_nt(qa, kall_ref[...])
    sn = _dot_nt(qa, knew_ref[...].astype(BF16))
    rc = lax.broadcasted_iota(jnp.int32, (nrow, n_kv * w), 0)
    cc = lax.broadcasted_iota(jnp.int32, (nrow, n_kv * w), 1)
    mask_c = (_div_pow2(cc, w) == _div_pow2(rc, group * rows)) & (_mod_pow2(cc, w) >= _mod_pow2(rc, rows))
    rn = lax.broadcasted_iota(jnp.int32, (nrow, n_kv * rows), 0)
    cn = lax.broadcasted_iota(jnp.int32, (nrow, n_kv * rows), 1)
    jn = _mod_pow2(cn, rows)
    mask_n = (_div_pow2(cn, rows) == _div_pow2(rn, group * rows)) & (jn <= _mod_pow2(rn, rows)) & (jn < valid)
    sc = jnp.where(mask_c, sc, -jnp.inf)
    sn = jnp.where(mask_n, sn, -jnp.inf)
    sink = sink_ref[...]
    m = jnp.maximum(jnp.maximum(jnp.max(sc, axis=-1, keepdims=True), jnp.max(sn, axis=-1, keepdims=True)), sink)
    pc = jnp.exp(sc - m)
    pn = jnp.exp(sn - m)
    denom = jnp.sum(pc, axis=-1, keepdims=True) + jnp.sum(pn, axis=-1, keepdims=True) + jnp.exp(sink - m)
    o = (_dot(pc.astype(BF16), vall_ref[...]) + _dot(pn.astype(BF16), vnew_ref[...].astype(BF16))) / denom
    for hq in range(n_heads):
        o_ref[:, hq * hd:(hq + 1) * hd] = o[hq * rows:(hq + 1) * rows, :].astype(o_ref.dtype)


def _swa_sample(z, ck, cv, sinks, pos, o_prev, *, batch, row0, rows, valid, hd, n_rot):
    nq, nkv = SWA_HEADS * hd, SWA_KV_HEADS * hd
    w = ck.shape[1]
    a, b, c = _rope_tables(pos, hd, n_rot)
    kcol = nq // nkv
    rb0 = row0 // rows
    nrow = SWA_HEADS * rows
    sink_col = jnp.repeat(sinks, rows)[:, None]
    tab = pl.BlockSpec((rows, LANES), lambda bb: (0, 0))
    kern = functools.partial(_swa_sample_kernel, hd=hd, half=n_rot // 2, valid=valid)
    return pl.pallas_call(
        kern,
        grid=(batch,),
        in_specs=[
            pl.BlockSpec((rows, nq), lambda bb: (rb0 + bb, 0)),
            pl.BlockSpec((rows, nkv), lambda bb: (rb0 + bb, kcol)),
            pl.BlockSpec((rows, nkv), lambda bb: (rb0 + bb, kcol + 1)),
            pl.BlockSpec((None, w, nkv), lambda bb: (bb, 0, 0)),
            pl.BlockSpec((None, w, nkv), lambda bb: (bb, 0, 0)),
            tab, tab, tab,
            pl.BlockSpec((nrow, 1), lambda bb: (0, 0)),
            pl.BlockSpec(memory_space=pl.ANY),
        ],
        out_specs=[
            pl.BlockSpec((rows, nq), lambda bb: (rb0 + bb, 0)),
            pl.BlockSpec((None, rows, nkv), lambda bb: (bb, 0, 0)),
        ],
        out_shape=[
            jax.ShapeDtypeStruct((z.shape[0], nq), BF16),
            jax.ShapeDtypeStruct((batch, rows, nkv), F32),
        ],
        scratch_shapes=[pltpu.VMEM((nrow, hd), F32), pltpu.VMEM((SWA_KV_HEADS * w, hd), BF16),
                        pltpu.VMEM((SWA_KV_HEADS * w, hd), BF16), pltpu.VMEM((SWA_KV_HEADS * rows, hd), F32),
                        pltpu.VMEM((SWA_KV_HEADS * rows, hd), F32)],
        input_output_aliases={9: 0},
        compiler_params=_params("parallel"),
        name="swa_sample",
    )(z, z, z, ck, cv, a, b, c, sink_col, o_prev)


def _merge_rows(xp, xs, rows):
    n = xp.shape[-1]
    xs = jnp.pad(xs, ((0, 0), (0, rows - xs.shape[1]), (0, 0)))
    return jnp.concatenate([xp.reshape(-1, n), xs.reshape(-1, n)], axis=0)


def kernel(x_prompt, x_sample, state_gla, state_ret, cache_swa_k, cache_swa_v, p_prompt, p_sample, norm_mix, norm_mlp, norm_ple, norm_final, gla_w_in, gla_w_a2, gla_b_a, gla_norm, gla_w_out, ret_w_in, ret_norm, ret_w_out, swa_w_in, swa_b_in, swa_sinks, swa_w_out, swa_b_out, mlp_w_up, mlp_w_down, ple_w_gate, ple_w_proj):
    bp, lp, d = x_prompt.shape
    bs, ls, _ = x_sample.shape
    depth = norm_mix.shape[0]
    lpad = SUBLANES
    mp, ms = bp * lp, bs * lpad
    pos_p = jnp.arange(lp, dtype=jnp.int32)
    pos_s = PAST_LEN + jnp.arange(lpad, dtype=jnp.int32)

    x = _merge_rows(x_prompt, x_sample, lpad)
    p_all = jnp.stack([_merge_rows(p_prompt[i], p_sample[i], lpad) for i in range(depth)])
    gla_w_in_t = jnp.swapaxes(gla_w_in, 1, 2)

    gla_p = gla_s = ret_p = ret_s = None
    swa_kp, swa_vp, swa_ks, swa_vs = [], [], [], []
    h, ssq = _rmsnorm(x, norm_mix[0], BF16), None
    for i in range(depth):
        j = i // N_MIXERS
        if i % N_MIXERS == 0:
            rank = gla_w_a2.shape[1]
            nz = gla_w_in.shape[2] - rank
            z = _matmul(h, gla_w_in_t, j, n=nz, w_t=True, row_ssq=ssq, out_dtype=BF16, name="gla_in")
            alow = _matmul(h, gla_w_in_t, j, n=rank, col0=nz, w_t=True, row_ssq=ssq, out_dtype=F32, name="gla_in_lowrank")
            o, gla_p = _gla_recurrence(z, alow, gla_w_a2, gla_b_a, gla_norm, j, None, None, gla_p, batch=bp, length=lp,
                                       row0=0, rows=min(GLA_ROWS, lp), chunk=GLA_CHUNK, sub=GLA_SUB, valid=GLA_CHUNK)
            o, gla_s = _gla_recurrence(z, alow, gla_w_a2, gla_b_a, gla_norm, j, state_gla, o, gla_s, batch=bs,
                                       length=lpad, row0=mp, rows=lpad, chunk=lpad, sub=lpad, valid=ls)
            x, h, ssq = _matmul(o, gla_w_out, j, resid=x, norm_out=norm_mlp[i], out_dtype=F32, name="gla_out")
        elif i % N_MIXERS == 1:
            z = _matmul(h, ret_w_in, j, row_ssq=ssq, out_dtype=BF16, name="ret_in")
            cp = min(RET_CHUNK, lp)
            o, ret_p = _ret_recurrence(z, pos_p, ret_norm, j, None, None, ret_p, batch=bp, length=lp, row0=0,
                                       chunk=cp, valid=cp, hps=RET_PROMPT_HEADS_PER_STEP)
            o, ret_s = _ret_recurrence(z, pos_s, ret_norm, j, state_ret, o, ret_s, batch=bs, length=lpad, row0=mp,
                                       chunk=lpad, valid=ls, hps=RET_SAMPLE_HEADS_PER_STEP)
            x, h, ssq = _matmul(o, ret_w_out, j, resid=x, norm_out=norm_mlp[i], out_dtype=F32, name="ret_out")
        else:
            hd = d // SWA_HEADS
            n_rot = hd // 4
            nq, nkv = SWA_HEADS * hd, SWA_KV_HEADS * hd
            z = _matmul(h, swa_w_in, j, row_ssq=ssq, bias=swa_b_in, out_dtype=F32, name="swa_in")
            o, k_p = _swa_prompt(z, swa_sinks[j], batch=bp, length=lp, hd=hd, n_rot=n_rot)
            ck = cache_swa_k[j].reshape(bs, -1, nkv)
            cv = cache_swa_v[j].reshape(bs, -1, nkv)
            o, k_s = _swa_sample(z, ck, cv, swa_sinks[j], pos_s, o, batch=bs, row0=mp, rows=lpad, valid=ls, hd=hd, n_rot=n_rot)
            v_new = z[:, nq + nkv:]
            swa_kp.append(k_p.reshape(bp, SWA_WINDOW, SWA_KV_HEADS, hd))
            swa_vp.append(v_new[:mp].reshape(bp, lp, SWA_KV_HEADS, hd)[:, -SWA_WINDOW:])
            swa_ks.append(k_s[:, :ls].reshape(bs, ls, SWA_KV_HEADS, hd))
            swa_vs.append(v_new[mp:].reshape(bs, lpad, SWA_KV_HEADS, hd)[:, :ls])
            x, h, ssq = _matmul(o, swa_w_out, j, bias=swa_b_out, resid=x, norm_out=norm_mlp[i], out_dtype=F32,
                                name="swa_out")
        u = _matmul(h, mlp_w_up, i, row_ssq=ssq, act="relu2", out_dtype=BF16, name="mlp_up")
        x, h, ssq = _matmul(u, mlp_w_down, i, resid=x, norm_out=norm_ple[i], out_dtype=F32, name="mlp_down")
        if i + 1 < depth:
            x, h, ssq = _matmul(h, ple_w_gate, i, row_ssq=ssq, gate=(p_all[i], ple_w_proj), resid=x,
                                norm_out=norm_mix[i + 1], out_dtype=F32, name="ple")
        else:
            x = _matmul(h, ple_w_gate, i, row_ssq=ssq, gate=(p_all[i], ple_w_proj), resid=x, out_dtype=F32, name="ple")
    y_p = _rmsnorm(x, norm_final, F32, row0=0, nrows=mp)
    y_s = _rmsnorm(x, norm_final, F32, row0=mp, nrows=ms)
    return (y_p.reshape(bp, lp, d), y_s.reshape(bs, lpad, d)[:, :ls],
            gla_p, gla_s, ret_p, ret_s,
            jnp.stack(swa_kp), jnp.stack(swa_vp), jnp.stack(swa_ks), jnp.stack(swa_vs))
```
